```python
import jax, jax.numpy as jnp
from jax import lax
import numpy as np

D_MODEL = 1024
BATCH = 16
SEQ = 2048
DEPTH = 1
DEC_BATCH = 8
DEC_SEQ = 8192
PAST_LEN = 128

N_META = 16
GRID_W = 64
HEAD_DIM = 64
ATTN_WIDTH = D_MODEL // 2
N_Q_HEADS = ATTN_WIDTH // HEAD_DIM
N_KV_HEADS = N_Q_HEADS // 4
Q_PER_KV = N_Q_HEADS // N_KV_HEADS
KV_WIDTH = N_KV_HEADS * HEAD_DIM
FOURIER_WIDTH = D_MODEL // 2
N_FOURIER_GROUPS = 4
FOURIER_GROUP = FOURIER_WIDTH // N_FOURIER_GROUPS
D_FF = -(-(8 * D_MODEL) // (3 * 256)) * 256
IN_WIDTH = ATTN_WIDTH + 2 * KV_WIDTH + FOURIER_WIDTH + 2 * D_MODEL
ROPE_AXIS_DIM = HEAD_DIM // 2
ROPE_THETA = 10000.0
Q_BLOCK = 128
EPS = 1e-6

kernel_name = "hybrid_fnet_gqa_axial_rope_encoder"


def rms_norm(x, g):
    xf = x.astype(jnp.float32)
    y = xf * lax.rsqrt(jnp.mean(xf * xf, axis=-1, keepdims=True) + EPS) * g.astype(jnp.float32)
    return y.astype(x.dtype)


def grid_positions(n_tokens):
    rows = n_tokens // GRID_W
    meta_row = jnp.full((N_META,), -1.0, jnp.float32)
    meta_col = jnp.arange(N_META, dtype=jnp.float32)
    tok_row = jnp.repeat(jnp.arange(rows, dtype=jnp.float32), GRID_W)
    tok_col = jnp.tile(jnp.arange(GRID_W, dtype=jnp.float32), rows)
    return jnp.concatenate([meta_row, tok_row]), jnp.concatenate([meta_col, tok_col])


def rope_1d(x, ang):
    half = ROPE_AXIS_DIM // 2
    x1, x2 = x[..., :half], x[..., half:]
    c = jnp.cos(ang)[None, :, None, :]
    s = jnp.sin(ang)[None, :, None, :]
    return jnp.concatenate([x1 * c - x2 * s, x2 * c + x1 * s], axis=-1)


def axial_rope(x, row, col):
    inv_freq = 1.0 / (ROPE_THETA ** (jnp.arange(0, ROPE_AXIS_DIM, 2, dtype=jnp.float32) / ROPE_AXIS_DIM))
    ang_r = row[:, None] * inv_freq[None, :]
    ang_c = col[:, None] * inv_freq[None, :]
    xf = x.astype(jnp.float32)
    out = jnp.concatenate([rope_1d(xf[..., :ROPE_AXIS_DIM], ang_r),
                           rope_1d(xf[..., ROPE_AXIS_DIM:], ang_c)], axis=-1)
    return out.astype(x.dtype)


def head_rms_norm(x, g):
    xf = x.astype(jnp.float32)
    y = xf * lax.rsqrt(jnp.mean(xf * xf, axis=-1, keepdims=True) + EPS) * g.astype(jnp.float32)
    return y.astype(x.dtype)


def attention_block(qb, k, v):
    scale = HEAD_DIM ** -0.5
    s = jnp.einsum('bqhgd,bshd->bhgqs', qb, k).astype(jnp.float32) * scale
    p = jax.nn.softmax(s, axis=-1).astype(v.dtype)
    return jnp.einsum('bhgqs,bshd->bqhgd', p, v)


def bidirectional_gqa(q, k, v):
    b = q.shape[0]
    out_meta = attention_block(q[:, :N_META], k, v)
    q_real = q[:, N_META:]
    n = q_real.shape[1]
    nb = n // Q_BLOCK
    q_blocks = q_real.reshape(b, nb, Q_BLOCK, N_KV_HEADS, Q_PER_KV, HEAD_DIM).swapaxes(0, 1)
    out_blocks = lax.map(lambda qb: attention_block(qb, k, v), q_blocks)
    out_real = out_blocks.swapaxes(0, 1).reshape(b, n, N_KV_HEADS, Q_PER_KV, HEAD_DIM)
    out = jnp.concatenate([out_meta, out_real], axis=1)
    return out.reshape(b, out.shape[1], ATTN_WIDTH)


def fourier_mix(f):
    b, l, _ = f.shape
    fg = f.astype(jnp.float32).reshape(b, l, N_FOURIER_GROUPS, FOURIER_GROUP)
    y = jnp.real(jnp.fft.fftn(fg, axes=(1, 3), norm='ortho'))
    return y.reshape(b, l, FOURIER_WIDTH).astype(f.dtype)


def encoder_layer(x, row, col, norm_mix_g, w_in, q_norm_g, k_norm_g, w_attn_o, w_four_o, w_out,
                  norm_ffn_g, w_ffn_in, w_ffn_out):
    b, l, _ = x.shape
    h = rms_norm(x, norm_mix_g)
    z = h @ w_in
    o0 = ATTN_WIDTH
    o1 = o0 + KV_WIDTH
    o2 = o1 + KV_WIDTH
    o3 = o2 + FOURIER_WIDTH
    o4 = o3 + D_MODEL
    q = z[..., :o0].reshape(b, l, N_Q_HEADS, HEAD_DIM)
    k = z[..., o0:o1].reshape(b, l, N_KV_HEADS, HEAD_DIM)
    v = z[..., o1:o2].reshape(b, l, N_KV_HEADS, HEAD_DIM)
    f = z[..., o2:o3]
    g_attn = jax.nn.sigmoid(z[..., o3:o4])
    g_four = jax.nn.sigmoid(z[..., o4:])
    q = axial_rope(head_rms_norm(q, q_norm_g), row, col)
    k = axial_rope(head_rms_norm(k, k_norm_g), row, col)
    q = q.reshape(b, l, N_KV_HEADS, Q_PER_KV, HEAD_DIM)
    a = bidirectional_gqa(q, k, v) @ w_attn_o
    fo = fourier_mix(f) @ w_four_o
    merged = g_attn * a + g_four * fo
    x = x + merged @ w_out
    h2 = rms_norm(x, norm_ffn_g)
    u = h2 @ w_ffn_in
    gate, up = u[..., :D_FF], u[..., D_FF:]
    x = x + (jax.nn.silu(gate) * up) @ w_ffn_out
    return x


def encode(x, meta_tokens, norm_mix_g, w_in, q_norm_g, k_norm_g, w_attn_o, w_four_o, w_out,
           norm_ffn_g, w_ffn_in, w_ffn_out, final_norm_g):
    b, n, d = x.shape
    row, col = grid_positions(n)
    meta = jnp.broadcast_to(meta_tokens.astype(x.dtype)[None], (b, N_META, d))
    h = jnp.concatenate([meta, x], axis=1)
    for layer in range(DEPTH):
        h = encoder_layer(h, row, col, norm_mix_g[layer], w_in[layer], q_norm_g[layer], k_norm_g[layer],
                          w_attn_o[layer], w_four_o[layer], w_out[layer], norm_ffn_g[layer],
                          w_ffn_in[layer], w_ffn_out[layer])
    h = rms_norm(h, final_norm_g)
    return h[:, N_META:]


def setup_inputs(seed: int = 0) -> dict:
    key = jax.random.key(seed)
    ks = jax.random.split(key, 14)

    def w(k, shape, fan_in, extra=1.0):
        return jax.random.normal(k, shape, jnp.float32) * (fan_in ** -0.5) * extra

    def gain(k, shape):
        return 1.0 + 0.02 * jax.random.normal(k, shape, jnp.float32)

    return {
        "x_prompt": jax.random.normal(ks[0], (BATCH, SEQ, D_MODEL), jnp.float32),
        "x_sample": jax.random.normal(ks[1], (DEC_BATCH, DEC_SEQ, D_MODEL), jnp.float32),
        "meta_tokens": jax.random.normal(ks[2], (N_META, D_MODEL), jnp.float32),
        "norm_mix_g": gain(ks[3], (DEPTH, D_MODEL)),
        "w_in": w(ks[4], (DEPTH, D_MODEL, IN_WIDTH), D_MODEL),
        "q_norm_g": gain(ks[5], (DEPTH, HEAD_DIM)),
        "k_norm_g": gain(ks[6], (DEPTH, HEAD_DIM)),
        "w_attn_o": w(ks[7], (DEPTH, ATTN_WIDTH, D_MODEL), ATTN_WIDTH),
        "w_four_o": w(ks[8], (DEPTH, FOURIER_WIDTH, D_MODEL), FOURIER_WIDTH),
        "w_out": w(ks[9], (DEPTH, D_MODEL, D_MODEL), D_MODEL),
        "norm_ffn_g": gain(ks[10], (DEPTH, D_MODEL)),
        "w_ffn_in": w(ks[11], (DEPTH, D_MODEL, 2 * D_FF), D_MODEL),
        "w_ffn_out": w(ks[12], (DEPTH, D_FF, D_MODEL), D_FF),
        "final_norm_g": gain(ks[13], (D_MODEL,)),
    }


def reference(x_prompt, x_sample, meta_tokens, norm_mix_g, w_in, q_norm_g, k_norm_g, w_attn_o, w_four_o,
              w_out, norm_ffn_g, w_ffn_in, w_ffn_out, final_norm_g):
    y_prompt = encode(x_prompt, meta_tokens, norm_mix_g, w_in, q_norm_g, k_norm_g, w_attn_o, w_four_o,
                      w_out, norm_ffn_g, w_ffn_in, w_ffn_out, final_norm_g)
    y_sample = encode(x_sample, meta_tokens, norm_mix_g, w_in, q_norm_g, k_norm_g, w_attn_o, w_four_o,
                      w_out, norm_ffn_g, w_ffn_in, w_ffn_out, final_norm_g)
    return (y_prompt, y_sample)
```

```python
import functools
import math

import jax
import jax.numpy as jnp
from jax import lax
from jax.experimental import pallas as pl
from jax.experimental.pallas import tpu as pltpu

N_META = 16
GRID_W = 64
HEAD_DIM = 64
N_Q_HEADS = 8
N_KV_HEADS = 2
Q_PER_KV = N_Q_HEADS // N_KV_HEADS
ATTN_WIDTH = N_Q_HEADS * HEAD_DIM
KV_WIDTH = N_KV_HEADS * HEAD_DIM
FOURIER_WIDTH = 512
FOURIER_GROUP = 128
N_FOURIER_GROUPS = FOURIER_WIDTH // FOURIER_GROUP
ROPE_AXIS_DIM = HEAD_DIM // 2
ROPE_THETA = 10000.0
EPS = 1e-6

LANES = 128
VMEM_LIMIT_BYTES = 56 * 1024 * 1024

F32 = jnp.float32
BF16 = jnp.bfloat16


def _dot(a, b):
    return jnp.dot(a, b, preferred_element_type=F32)


def _const_spec(shape):
    zeros = (0,) * len(shape)
    return pl.BlockSpec(shape, lambda *_: zeros, pipeline_mode=pl.Buffered(1))


def _params(n_axes):
    return pltpu.CompilerParams(dimension_semantics=("arbitrary",) * n_axes,
                                vmem_limit_bytes=VMEM_LIMIT_BYTES)


def _in_proj_kernel(x_ref, g_ref, w_ref, cos_ref, sin_ref, gq_ref, gk_ref, hmean_ref, wc_ref,
                    qt_ref, k_ref, vt_ref, z_ref, gate_ref):
    x = x_ref[0]
    h = (x * lax.rsqrt(jnp.mean(x * x, axis=-1, keepdims=True) + EPS) * g_ref[...]).astype(BF16)
    cos = cos_ref[...]
    sin = sin_ref[...]
    lane = lax.broadcasted_iota(jnp.int32, cos.shape, 1)
    first_half = (lane % ROPE_AXIS_DIM) < (ROPE_AXIS_DIM // 2)

    def head_norm_rope(xc, gain):
        msq = _dot((xc * xc).astype(BF16), hmean_ref[...])
        y = xc * lax.rsqrt(msq + EPS) * gain
        half = ROPE_AXIS_DIM // 2
        partner = jnp.where(first_half, pltpu.roll(y, LANES - half, 1), pltpu.roll(y, half, 1))
        return y * cos + partner * sin

    o0 = ATTN_WIDTH
    o1 = o0 + KV_WIDTH
    o2 = o1 + KV_WIDTH
    o3 = o2 + FOURIER_WIDTH
    d_model = x.shape[-1]

    for c in range(ATTN_WIDTH // LANES):
        zq = _dot(h, w_ref[:, c * LANES:(c + 1) * LANES])
        yq = head_norm_rope(zq, gq_ref[...]) * (HEAD_DIM ** -0.5)
        qt_ref[0, c * LANES:(c + 1) * LANES, :] = yq.T.astype(BF16)

    yk = head_norm_rope(_dot(h, w_ref[:, o0:o1]), gk_ref[...])
    low = lane < HEAD_DIM
    k_ref[0, 0] = jnp.where(low, yk, 0.0).astype(BF16)
    k_ref[0, 1] = jnp.where(low, pltpu.roll(yk, HEAD_DIM, 1), 0.0).astype(BF16)

    vt_ref[0] = _dot(h, w_ref[:, o1:o2]).T.astype(BF16)

    for g in range(N_FOURIER_GROUPS):
        f = _dot(h, w_ref[:, o2 + g * FOURIER_GROUP:o2 + (g + 1) * FOURIER_GROUP]).astype(BF16)
        zc = _dot(f, wc_ref[...])
        z_ref[0, :, g * FOURIER_GROUP:(g + 1) * FOURIER_GROUP] = zc[:, :FOURIER_GROUP].astype(BF16)
        z_ref[0, :, FOURIER_WIDTH + g * FOURIER_GROUP:FOURIER_WIDTH + (g + 1) * FOURIER_GROUP] = (
            zc[:, FOURIER_GROUP:].astype(BF16))

    gate_chunk = 512
    for c in range(2 * d_model // gate_chunk):
        zg = _dot(h, w_ref[:, o3 + c * gate_chunk:o3 + (c + 1) * gate_chunk])
        gate_ref[0, :, c * gate_chunk:(c + 1) * gate_chunk] = (1.0 / (1.0 + jnp.exp(-zg))).astype(BF16)


def _in_proj(x, norm_g, w_in, cos, sin, gq, gk, hmean, wc, tm):
    b, n, d = x.shape
    in_width = w_in.shape[1]
    grid = (b, n // tm)
    out_shape = (
        jax.ShapeDtypeStruct((b, ATTN_WIDTH, n), BF16),
        jax.ShapeDtypeStruct((b, N_KV_HEADS, n, LANES), BF16),
        jax.ShapeDtypeStruct((b, KV_WIDTH, n), BF16),
        jax.ShapeDtypeStruct((b, n, 2 * FOURIER_WIDTH), BF16),
        jax.ShapeDtypeStruct((b, n, 2 * d), BF16),
    )
    return pl.pallas_call(
        _in_proj_kernel,
        grid=grid,
        in_specs=[
            pl.BlockSpec((1, tm, d), lambda i, j: (i, j, 0)),
            _const_spec((1, d)),
            _const_spec((d, in_width)),
            pl.BlockSpec((tm, LANES), lambda i, j: (j, 0)),
            pl.BlockSpec((tm, LANES), lambda i, j: (j, 0)),
            _const_spec((1, LANES)),
            _const_spec((1, LANES)),
            _const_spec((LANES, LANES)),
            _const_spec((FOURIER_GROUP, 2 * FOURIER_GROUP)),
        ],
        out_specs=(
            pl.BlockSpec((1, ATTN_WIDTH, tm), lambda i, j: (i, 0, j)),
            pl.BlockSpec((1, N_KV_HEADS, tm, LANES), lambda i, j: (i, 0, j, 0)),
            pl.BlockSpec((1, KV_WIDTH, tm), lambda i, j: (i, 0, j)),
            pl.BlockSpec((1, tm, 2 * FOURIER_WIDTH), lambda i, j: (i, j, 0)),
            pl.BlockSpec((1, tm, 2 * d), lambda i, j: (i, j, 0)),
        ),
        out_shape=out_shape,
        compiler_params=_params(2),
        name="in_proj",
    )(x, norm_g, w_in, cos, sin, gq, gk, hmean, wc)


def _attn_kernel(qt_ref, k_ref, vt_ref, km_ref, vtm_ref, ot_ref, *, tk):
    tq = qt_ref.shape[2]
    n_keys = k_ref.shape[2]
    qt = jnp.concatenate([qt_ref[0, h * HEAD_DIM:(h + 1) * HEAD_DIM, :] for h in range(Q_PER_KV)], axis=1)
    qt = jnp.concatenate([qt, jnp.zeros_like(qt)], axis=0)

    s = _dot(km_ref[0, 0], qt)
    m = jnp.max(s, axis=0, keepdims=True)
    p = jnp.exp(s - m)
    l = jnp.sum(p, axis=0, keepdims=True)
    acc = _dot(vtm_ref[0], p.astype(BF16))

    def body(j, carry):
        m, l, acc = carry
        start = pl.multiple_of(j * tk, tk)
        s = _dot(k_ref[0, 0, pl.ds(start, tk), :], qt)
        m_new = jnp.maximum(m, jnp.max(s, axis=0, keepdims=True))
        alpha = jnp.exp(m - m_new)
        p = jnp.exp(s - m_new)
        l = alpha * l + jnp.sum(p, axis=0, keepdims=True)
        acc = alpha * acc + _dot(vt_ref[0, :, pl.ds(start, tk)], p.astype(BF16))
        return m_new, l, acc

    m, l, acc = lax.fori_loop(0, n_keys // tk, body, (m, l, acc))
    out = (acc / l).astype(BF16)
    for h in range(Q_PER_KV):
        ot_ref[0, h * HEAD_DIM:(h + 1) * HEAD_DIM, :] = out[:, h * tq:(h + 1) * tq]


def _attention(qt, k, vt, k_meta, vt_meta, tq, tk):
    b, _, n = qt.shape
    group_rows = Q_PER_KV * HEAD_DIM
    return pl.pallas_call(
        functools.partial(_attn_kernel, tk=tk),
        grid=(b, N_KV_HEADS, n // tq),
        in_specs=[
            pl.BlockSpec((1, group_rows, tq), lambda i, g, j: (i, g, j)),
            pl.BlockSpec((1, 1, n, LANES), lambda i, g, j: (i, g, 0, 0)),
            pl.BlockSpec((1, HEAD_DIM, n), lambda i, g, j: (i, g, 0)),
            pl.BlockSpec((1, 1, N_META, LANES), lambda i, g, j: (0, g, 0, 0)),
            pl.BlockSpec((1, HEAD_DIM, N_META), lambda i, g, j: (0, g, 0)),
        ],
        out_specs=pl.BlockSpec((1, group_rows, tq), lambda i, g, j: (i, g, j)),
        out_shape=jax.ShapeDtypeStruct((b, ATTN_WIDTH, n), BF16),
        compiler_params=_params(3),
        name="attention",
    )(qt, k, vt, k_meta, vt_meta)


def _seq_dft_kernel(mc_ref, ms_ref, z_ref, cm_ref, sm_ref, zm_ref, out_ref, acc_ref):
    kk = pl.program_id(2)

    @pl.when(kk == 0)
    def _():
        zm = zm_ref[0]
        acc_ref[...] = (_dot(cm_ref[...], zm[:, :FOURIER_WIDTH]) + _dot(sm_ref[...], zm[:, FOURIER_WIDTH:]))

    z = z_ref[0]
    acc_ref[...] += _dot(mc_ref[...], z[:, :FOURIER_WIDTH]) + _dot(ms_ref[...], z[:, FOURIER_WIDTH:])

    @pl.when(kk == pl.num_programs(2) - 1)
    def _():
        out_ref[0] = acc_ref[...].astype(out_ref.dtype)


def _seq_dft(z, z_meta, mc, ms, cm, sm, tm, tk):
    b, n, _ = z.shape
    return pl.pallas_call(
        _seq_dft_kernel,
        grid=(n // tm, b, n // tk),
        in_specs=[
            pl.BlockSpec((tm, tk), lambda r, i, kk: (r, kk)),
            pl.BlockSpec((tm, tk), lambda r, i, kk: (r, kk)),
            pl.BlockSpec((1, tk, 2 * FOURIER_WIDTH), lambda r, i, kk: (i, kk, 0)),
            pl.BlockSpec((tm, N_META), lambda r, i, kk: (r, 0)),
            pl.BlockSpec((tm, N_META), lambda r, i, kk: (r, 0)),
            pl.BlockSpec((1, N_META, 2 * FOURIER_WIDTH), lambda r, i, kk: (0, 0, 0)),
        ],
        out_specs=pl.BlockSpec((1, tm, FOURIER_WIDTH), lambda r, i, kk: (i, r, 0)),
        out_shape=jax.ShapeDtypeStruct((b, n, FOURIER_WIDTH), BF16),
        scratch_shapes=[pltpu.VMEM((tm, FOURIER_WIDTH), F32)],
        compiler_params=_params(3),
        name="seq_dft",
    )(mc, ms, z, cm, sm, z_meta)


def _rms(x, g):
    return x * lax.rsqrt(jnp.mean(x * x, axis=-1, keepdims=True) + EPS) * g


def _mix_ffn_kernel(x_ref, ot_ref, four_ref, gate_ref, wao_ref, wfo_ref, wout_ref, g2_ref, wfi_ref, wfo2_ref,
                    gfin_ref, y_ref, *, ff_chunk):
    x = x_ref[0]
    d_model = x.shape[-1]
    d_ff = wfo2_ref.shape[0]
    attn = ot_ref[0].astype(F32).T.astype(BF16)
    a = _dot(attn, wao_ref[...])
    fo = _dot(four_ref[0], wfo_ref[...])
    gates = gate_ref[0]
    merged = gates[:, :d_model].astype(F32) * a + gates[:, d_model:].astype(F32) * fo
    x1 = x + _dot(merged.astype(BF16), wout_ref[...])
    h2 = _rms(x1, g2_ref[...]).astype(BF16)
    acc = x1
    for c in range(d_ff // ff_chunk):
        gt = _dot(h2, wfi_ref[:, c * ff_chunk:(c + 1) * ff_chunk])
        up = _dot(h2, wfi_ref[:, d_ff + c * ff_chunk:d_ff + (c + 1) * ff_chunk])
        act = gt * (1.0 / (1.0 + jnp.exp(-gt))) * up
        acc = acc + _dot(act.astype(BF16), wfo2_ref[c * ff_chunk:(c + 1) * ff_chunk, :])
    y_ref[0] = _rms(acc, gfin_ref[...])


def _mix_ffn(x, ot, four, gates, w_attn_o, w_four_o, w_out, g2, w_ffn_in, w_ffn_out, gfin, tm, ff_chunk):
    b, n, d = x.shape
    d_ff = w_ffn_out.shape[0]
    return pl.pallas_call(
        functools.partial(_mix_ffn_kernel, ff_chunk=ff_chunk),
        grid=(b, n // tm),
        in_specs=[
            pl.BlockSpec((1, tm, d), lambda i, j: (i, j, 0)),
            pl.BlockSpec((1, ATTN_WIDTH, tm), lambda i, j: (i, 0, j)),
            pl.BlockSpec((1, tm, FOURIER_WIDTH), lambda i, j: (i, j, 0)),
            pl.BlockSpec((1, tm, 2 * d), lambda i, j: (i, j, 0)),
            _const_spec((ATTN_WIDTH, d)),
            _const_spec((FOURIER_WIDTH, d)),
            _const_spec((d, d)),
            _const_spec((1, d)),
            _const_spec((d, 2 * d_ff)),
            _const_spec((d_ff, d)),
            _const_spec((1, d)),
        ],
        out_specs=pl.BlockSpec((1, tm, d), lambda i, j: (i, j, 0)),
        out_shape=jax.ShapeDtypeStruct((b, n, d), F32),
        compiler_params=_params(2),
        name="mix_ffn",
    )(x, ot, four, gates, w_attn_o, w_four_o, w_out, g2, w_ffn_in, w_ffn_out, gfin)


def _rope_tables(row, col):
    inv_freq = 1.0 / (ROPE_THETA ** (jnp.arange(0, ROPE_AXIS_DIM, 2, dtype=F32) / ROPE_AXIS_DIM))
    ar = row[:, None] * inv_freq[None, :]
    ac = col[:, None] * inv_freq[None, :]
    cos = jnp.concatenate([jnp.cos(ar), jnp.cos(ar), jnp.cos(ac), jnp.cos(ac)], axis=-1)
    sin = jnp.concatenate([-jnp.sin(ar), jnp.sin(ar), -jnp.sin(ac), jnp.sin(ac)], axis=-1)
    reps = LANES // HEAD_DIM
    return jnp.tile(cos, (1, reps)), jnp.tile(sin, (1, reps))


def _unit_circle(prod, length):
    ang = (prod % length).astype(F32) * (2.0 * math.pi / length)
    return jnp.cos(ang), jnp.sin(ang)


def _seq_dft_tables(n):
    length = n + N_META
    pos = jnp.concatenate([jnp.arange(n, dtype=jnp.int32) + N_META, jnp.arange(N_META, dtype=jnp.int32)])
    coarse = (jnp.arange(n // GRID_W, dtype=jnp.int32) * GRID_W)[:, None] * pos[None, :]
    fine = (jnp.arange(GRID_W, dtype=jnp.int32) + N_META)[:, None] * pos[None, :]
    c1, s1 = _unit_circle(coarse, length)
    c2, s2 = _unit_circle(fine, length)
    cos = (c1[:, None, :] * c2[None, :, :] - s1[:, None, :] * s2[None, :, :]).reshape(n, length).astype(BF16)
    sin = (s1[:, None, :] * c2[None, :, :] + c1[:, None, :] * s2[None, :, :]).reshape(n, length).astype(BF16)
    return cos[:, :n], sin[:, :n], cos[:, n:], sin[:, n:]


def _channel_dft_table(length):
    idx = jnp.arange(FOURIER_GROUP, dtype=jnp.int32)
    c, s = _unit_circle(idx[:, None] * idx[None, :], FOURIER_GROUP)
    scale = 1.0 / math.sqrt(length * FOURIER_GROUP)
    return (jnp.concatenate([c, -s], axis=1) * scale).astype(BF16)


def _encode(x, meta_tokens, p):
    b, n, d = x.shape
    length = n + N_META
    tok = jnp.arange(n, dtype=jnp.int32)
    cos, sin = _rope_tables((tok // GRID_W).astype(F32), (tok % GRID_W).astype(F32))
    meta_pad = LANES
    mrow = jnp.full((meta_pad,), -1.0, F32)
    mcol = jnp.arange(meta_pad, dtype=F32)
    cos_m, sin_m = _rope_tables(mrow, mcol)
    wc = _channel_dft_table(length)
    x_meta = jnp.zeros((1, meta_pad, d), F32).at[0, :N_META].set(meta_tokens)

    proj_consts = (p["norm_mix_g"], p["w_in"])
    head_consts = (p["gq"], p["gk"], p["hmean"], wc)
    _, k_m, vt_m, z_m, _ = _in_proj(x_meta, *proj_consts, cos_m, sin_m, *head_consts, tm=meta_pad)
    qt, k, vt, z, gates = _in_proj(x, *proj_consts, cos, sin, *head_consts, tm=min(n, 512))

    ot = _attention(qt, k, vt, k_m[:, :, :N_META], vt_m[:, :, :N_META], tq=min(n, 256), tk=min(n, 512))

    mc, ms, cm, sm = _seq_dft_tables(n)
    four = _seq_dft(z, z_m[:, :N_META], mc, ms, cm, sm, tm=min(n, 2048), tk=min(n, 1024))

    return _mix_ffn(x, ot, four, gates, p["w_attn_o"], p["w_four_o"], p["w_out"], p["norm_ffn_g"],
                    p["w_ffn_in"], p["w_ffn_out"], p["final_norm_g"], tm=min(n, 256), ff_chunk=1408)


def kernel(x_prompt, x_sample, meta_tokens, norm_mix_g, w_in, q_norm_g, k_norm_g, w_attn_o, w_four_o, w_out,
           norm_ffn_g, w_ffn_in, w_ffn_out, final_norm_g):
    assert w_in.shape[0] == 1, "single-layer encoder: meta rows are only needed as keys / DFT inputs"
    reps = LANES // HEAD_DIM
    head_id = jnp.arange(LANES) // HEAD_DIM
    p = {
        "norm_mix_g": norm_mix_g[0][None, :],
        "w_in": w_in[0].astype(BF16),
        "gq": jnp.tile(q_norm_g[0], reps)[None, :],
        "gk": jnp.tile(k_norm_g[0], reps)[None, :],
        "hmean": ((head_id[:, None] == head_id[None, :]).astype(F32) / HEAD_DIM).astype(BF16),
        "w_attn_o": w_attn_o[0].astype(BF16),
        "w_four_o": w_four_o[0].astype(BF16),
        "w_out": w_out[0].astype(BF16),
        "norm_ffn_g": norm_ffn_g[0][None, :],
        "w_ffn_in": w_ffn_in[0].astype(BF16),
        "w_ffn_out": w_ffn_out[0].astype(BF16),
        "final_norm_g": final_norm_g[None, :],
    }
    return (_encode(x_prompt, meta_tokens, p), _encode(x_sample, meta_tokens, p))
```

```python
import functools
import math

import jax
import jax.numpy as jnp
from jax import lax
from jax.experimental import pallas as pl
from jax.experimental.pallas import tpu as pltpu

N_META = 16
GRID_W = 64
HEAD_DIM = 64
N_Q_HEADS = 8
N_KV_HEADS = 2
Q_PER_KV = N_Q_HEADS // N_KV_HEADS
ATTN_WIDTH = N_Q_HEADS * HEAD_DIM
KV_WIDTH = N_KV_HEADS * HEAD_DIM
FOURIER_WIDTH = 512
FOURIER_GROUP = 128
N_FOURIER_GROUPS = FOURIER_WIDTH // FOURIER_GROUP
ROPE_AXIS_DIM = HEAD_DIM // 2
ROPE_THETA = 10000.0
EPS = 1e-6

LANES = 128
VMEM_LIMIT_BYTES = 56 * 1024 * 1024

F32 = jnp.float32
BF16 = jnp.bfloat16

LOG2E = math.log2(math.e)
Q_SCALE = HEAD_DIM ** -0.5 * LOG2E
MAX_SCORE_BOUND_LOG2 = 60.0


def _dot(a, b):
    return jnp.dot(a, b, preferred_element_type=F32)


def _const_spec(shape):
    zeros = (0,) * len(shape)
    return pl.BlockSpec(shape, lambda *_: zeros, pipeline_mode=pl.Buffered(1))


def _params(n_axes):
    return pltpu.CompilerParams(dimension_semantics=("arbitrary",) * n_axes,
                                vmem_limit_bytes=VMEM_LIMIT_BYTES)


def _in_proj_kernel(x_ref, g_ref, w_ref, cos_ref, sin_ref, gq_ref, gk_ref, kbias_ref, hmean_ref, wc_ref,
                    qt_ref, k_ref, vt_ref, z_ref, gate_ref):
    x = x_ref[0]
    h = (x * lax.rsqrt(jnp.mean(x * x, axis=-1, keepdims=True) + EPS) * g_ref[...]).astype(BF16)
    cos = cos_ref[...]
    sin = sin_ref[...]
    lane = lax.broadcasted_iota(jnp.int32, cos.shape, 1)
    first_half = (lane % ROPE_AXIS_DIM) < (ROPE_AXIS_DIM // 2)

    def head_norm_rope(xc, gain):
        msq = _dot((xc * xc).astype(BF16), hmean_ref[...])
        y = xc * lax.rsqrt(msq + EPS) * gain
        half = ROPE_AXIS_DIM // 2
        partner = jnp.where(first_half, pltpu.roll(y, LANES - half, 1), pltpu.roll(y, half, 1))
        return y * cos + partner * sin

    o0 = ATTN_WIDTH
    o1 = o0 + KV_WIDTH
    o2 = o1 + KV_WIDTH
    o3 = o2 + FOURIER_WIDTH
    d_model = x.shape[-1]

    for c in range(ATTN_WIDTH // LANES):
        zq = _dot(h, w_ref[:, c * LANES:(c + 1) * LANES])
        yq = head_norm_rope(zq, gq_ref[...]) * Q_SCALE
        qt_ref[0, c * LANES:(c + 1) * LANES, :] = yq.T.astype(BF16)

    yk = head_norm_rope(_dot(h, w_ref[:, o0:o1]), gk_ref[...])
    low = lane < HEAD_DIM
    kbias = kbias_ref[...]
    k_ref[0, 0] = jnp.where(low, yk, kbias).astype(BF16)
    k_ref[0, 1] = jnp.where(low, pltpu.roll(yk, HEAD_DIM, 1), kbias).astype(BF16)

    vt_ref[0] = _dot(h, w_ref[:, o1:o2]).T.astype(BF16)

    for g in range(N_FOURIER_GROUPS):
        f = _dot(h, w_ref[:, o2 + g * FOURIER_GROUP:o2 + (g + 1) * FOURIER_GROUP]).astype(BF16)
        zc = _dot(f, wc_ref[...])
        z_ref[0, :, g * FOURIER_GROUP:(g + 1) * FOURIER_GROUP] = zc[:, :FOURIER_GROUP].astype(BF16)
        z_ref[0, :, FOURIER_WIDTH + g * FOURIER_GROUP:FOURIER_WIDTH + (g + 1) * FOURIER_GROUP] = (
            zc[:, FOURIER_GROUP:].astype(BF16))

    gate_chunk = 512
    for c in range(2 * d_model // gate_chunk):
        zg = _dot(h, w_ref[:, o3 + c * gate_chunk:o3 + (c + 1) * gate_chunk])
        gate_ref[0, :, c * gate_chunk:(c + 1) * gate_chunk] = (1.0 / (1.0 + jnp.exp(-zg))).astype(BF16)


def _in_proj(x, norm_g, w_in, cos, sin, gq, gk, kbias, hmean, wc, tm):
    b, n, d = x.shape
    in_width = w_in.shape[1]
    grid = (b, n // tm)
    out_shape = (
        jax.ShapeDtypeStruct((b, ATTN_WIDTH, n), BF16),
        jax.ShapeDtypeStruct((b, N_KV_HEADS, n, LANES), BF16),
        jax.ShapeDtypeStruct((b, KV_WIDTH, n), BF16),
        jax.ShapeDtypeStruct((b, n, 2 * FOURIER_WIDTH), BF16),
        jax.ShapeDtypeStruct((b, n, 2 * d), BF16),
    )
    return pl.pallas_call(
        _in_proj_kernel,
        grid=grid,
        in_specs=[
            pl.BlockSpec((1, tm, d), lambda i, j: (i, j, 0)),
            _const_spec((1, d)),
            _const_spec((d, in_width)),
            pl.BlockSpec((tm, LANES), lambda i, j: (j, 0)),
            pl.BlockSpec((tm, LANES), lambda i, j: (j, 0)),
            _const_spec((1, LANES)),
            _const_spec((1, LANES)),
            _const_spec((1, LANES)),
            _const_spec((LANES, LANES)),
            _const_spec((FOURIER_GROUP, 2 * FOURIER_GROUP)),
        ],
        out_specs=(
            pl.BlockSpec((1, ATTN_WIDTH, tm), lambda i, j: (i, 0, j)),
            pl.BlockSpec((1, N_KV_HEADS, tm, LANES), lambda i, j: (i, 0, j, 0)),
            pl.BlockSpec((1, KV_WIDTH, tm), lambda i, j: (i, 0, j)),
            pl.BlockSpec((1, tm, 2 * FOURIER_WIDTH), lambda i, j: (i, j, 0)),
            pl.BlockSpec((1, tm, 2 * d), lambda i, j: (i, j, 0)),
        ),
        out_shape=out_shape,
        compiler_params=_params(2),
        name="in_proj",
    )(x, norm_g, w_in, cos, sin, gq, gk, kbias, hmean, wc)


def _attn_kernel(qt_ref, k_ref, vt_ref, km_ref, vtm_ref, ot_ref, *, tk, bounded):
    tq = qt_ref.shape[2]
    n_keys = k_ref.shape[2]
    n_blocks = n_keys // tk
    ones_row = (lax.broadcasted_iota(jnp.int32, (HEAD_DIM, tq), 0) == 0).astype(BF16)
    qts = [jnp.concatenate([qt_ref[0, h * HEAD_DIM:(h + 1) * HEAD_DIM, :], ones_row], axis=0)
           for h in range(Q_PER_KV)]

    if bounded:
        heads = range(Q_PER_KV)
        p = [jnp.exp2(_dot(km_ref[0, 0], qts[h])) for h in heads]
        l = [jnp.sum(p[h], axis=0, keepdims=True) for h in heads]
        acc = [_dot(vtm_ref[0], p[h].astype(BF16)) for h in heads]
        s = [_dot(k_ref[0, 0, 0:tk, :], qts[h]) for h in heads]
        for j in range(n_blocks):
            for h in heads:
                ph = jnp.exp2(s[h])
                l[h] = l[h] + jnp.sum(ph, axis=0, keepdims=True)
                if j + 1 < n_blocks:
                    s[h] = _dot(k_ref[0, 0, (j + 1) * tk:(j + 2) * tk, :], qts[h])
                acc[h] = acc[h] + _dot(vt_ref[0, :, j * tk:(j + 1) * tk], ph.astype(BF16))
        for h in heads:
            ot_ref[0, h * HEAD_DIM:(h + 1) * HEAD_DIM, :] = (acc[h] / l[h]).astype(BF16)
        return

    qt = jnp.concatenate(qts, axis=1)

    def block(kb, vb, m, l, acc):
        s = _dot(kb, qt)
        m_new = jnp.maximum(m, jnp.max(s, axis=0, keepdims=True))
        alpha = jnp.exp2(m - m_new)
        p = jnp.exp2(s - m_new)
        return (m_new, alpha * l + jnp.sum(p, axis=0, keepdims=True), alpha * acc + _dot(vb, p.astype(BF16)))

    width = qt.shape[1]
    init = (jnp.full((1, width), -jnp.inf, F32), jnp.zeros((1, width), F32), jnp.zeros((HEAD_DIM, width), F32))
    carry = block(km_ref[0, 0], vtm_ref[0], *init)

    def body(j, carry):
        start = pl.multiple_of(j * tk, tk)
        return block(k_ref[0, 0, pl.ds(start, tk), :], vt_ref[0, :, pl.ds(start, tk)], *carry)

    _, l, acc = lax.fori_loop(0, n_blocks, body, carry)
    out = (acc / l).astype(BF16)
    for h in range(Q_PER_KV):
        ot_ref[0, h * HEAD_DIM:(h + 1) * HEAD_DIM, :] = out[:, h * tq:(h + 1) * tq]


def _attention(qt, k, vt, k_meta, vt_meta, tq, tk, bounded):
    b, _, n = qt.shape
    group_rows = Q_PER_KV * HEAD_DIM
    return pl.pallas_call(
        functools.partial(_attn_kernel, tk=tk, bounded=bounded),
        grid=(b, N_KV_HEADS, n // tq),
        in_specs=[
            pl.BlockSpec((1, group_rows, tq), lambda i, g, j: (i, g, j)),
            pl.BlockSpec((1, 1, n, LANES), lambda i, g, j: (i, g, 0, 0)),
            pl.BlockSpec((1, HEAD_DIM, n), lambda i, g, j: (i, g, 0)),
            pl.BlockSpec((1, 1, N_META, LANES), lambda i, g, j: (0, g, 0, 0)),
            pl.BlockSpec((1, HEAD_DIM, N_META), lambda i, g, j: (0, g, 0)),
        ],
        out_specs=pl.BlockSpec((1, group_rows, tq), lambda i, g, j: (i, g, j)),
        out_shape=jax.ShapeDtypeStruct((b, ATTN_WIDTH, n), BF16),
        compiler_params=_params(3),
        name="attention",
    )(qt, k, vt, k_meta, vt_meta)


def _seq_dft_kernel(mc_ref, ms_ref, z_ref, cm_ref, sm_ref, zm_ref, out_ref, acc_ref):
    kk = pl.program_id(2)

    @pl.when(kk == 0)
    def _():
        zm = zm_ref[0]
        acc_ref[...] = (_dot(cm_ref[...], zm[:, :FOURIER_WIDTH]) + _dot(sm_ref[...], zm[:, FOURIER_WIDTH:]))

    z = z_ref[0]
    acc_ref[...] += _dot(mc_ref[...], z[:, :FOURIER_WIDTH]) + _dot(ms_ref[...], z[:, FOURIER_WIDTH:])

    @pl.when(kk == pl.num_programs(2) - 1)
    def _():
        out_ref[0] = acc_ref[...].astype(out_ref.dtype)


def _seq_dft(z, z_meta, mc, ms, cm, sm, tm, tk):
    b, n, _ = z.shape
    return pl.pallas_call(
        _seq_dft_kernel,
        grid=(n // tm, b, n // tk),
        in_specs=[
            pl.BlockSpec((tm, tk), lambda r, i, kk: (r, kk)),
            pl.BlockSpec((tm, tk), lambda r, i, kk: (r, kk)),
            pl.BlockSpec((1, tk, 2 * FOURIER_WIDTH), lambda r, i, kk: (i, kk, 0)),
            pl.BlockSpec((tm, N_META), lambda r, i, kk: (r, 0)),
            pl.BlockSpec((tm, N_META), lambda r, i, kk: (r, 0)),
            pl.BlockSpec((1, N_META, 2 * FOURIER_WIDTH), lambda r, i, kk: (0, 0, 0)),
        ],
        out_specs=pl.BlockSpec((1, tm, FOURIER_WIDTH), lambda r, i, kk: (i, r, 0)),
        out_shape=jax.ShapeDtypeStruct((b, n, FOURIER_WIDTH), BF16),
        scratch_shapes=[pltpu.VMEM((tm, FOURIER_WIDTH), F32)],
        compiler_params=_params(3),
        name="seq_dft",
    )(mc, ms, z, cm, sm, z_meta)


def _rms(x, g):
    return x * lax.rsqrt(jnp.mean(x * x, axis=-1, keepdims=True) + EPS) * g


def _mix_ffn_kernel(x_ref, ot_ref, four_ref, gate_ref, wao_ref, wfo_ref, wout_ref, g2_ref, wfi_ref, wfo2_ref,
                    gfin_ref, y_ref, *, ff_chunk):
    x = x_ref[0]
    d_model = x.shape[-1]
    d_ff = wfo2_ref.shape[0]
    attn = ot_ref[0].astype(F32).T.astype(BF16)
    a = _dot(attn, wao_ref[...])
    fo = _dot(four_ref[0], wfo_ref[...])
    gates = gate_ref[0]
    merged = gates[:, :d_model].astype(F32) * a + gates[:, d_model:].astype(F32) * fo
    x1 = x + _dot(merged.astype(BF16), wout_ref[...])
    h2 = _rms(x1, g2_ref[...]).astype(BF16)
    acc = x1
    for c in range(d_ff // ff_chunk):
        gt = _dot(h2, wfi_ref[:, c * ff_chunk:(c + 1) * ff_chunk])
        up = _dot(h2, wfi_ref[:, d_ff + c * ff_chunk:d_ff + (c + 1) * ff_chunk])
        act = gt * (1.0 / (1.0 + jnp.exp(-gt))) * up
        acc = acc + _dot(act.astype(BF16), wfo2_ref[c * ff_chunk:(c + 1) * ff_chunk, :])
    y_ref[0] = _rms(acc, gfin_ref[...])


def _mix_ffn(x, ot, four, gates, w_attn_o, w_four_o, w_out, g2, w_ffn_in, w_ffn_out, gfin, tm, ff_chunk):
    b, n, d = x.shape
    d_ff = w_ffn_out.shape[0]
    return pl.pallas_call(
        functools.partial(_mix_ffn_kernel, ff_chunk=ff_chunk),
        grid=(b, n // tm),
        in_specs=[
            pl.BlockSpec((1, tm, d), lambda i, j: (i, j, 0)),
            pl.BlockSpec((1, ATTN_WIDTH, tm), lambda i, j: (i, 0, j)),
            pl.BlockSpec((1, tm, FOURIER_WIDTH), lambda i, j: (i, j, 0)),
            pl.BlockSpec((1, tm, 2 * d), lambda i, j: (i, j, 0)),
            _const_spec((ATTN_WIDTH, d)),
            _const_spec((FOURIER_WIDTH, d)),
            _const_spec((d, d)),
            _const_spec((1, d)),
            _const_spec((d, 2 * d_ff)),
            _const_spec((d_ff, d)),
            _const_spec((1, d)),
        ],
        out_specs=pl.BlockSpec((1, tm, d), lambda i, j: (i, j, 0)),
        out_shape=jax.ShapeDtypeStruct((b, n, d), F32),
        compiler_params=_params(2),
        name="mix_ffn",
    )(x, ot, four, gates, w_attn_o, w_four_o, w_out, g2, w_ffn_in, w_ffn_out, gfin)


def _rope_tables(row, col):
    inv_freq = 1.0 / (ROPE_THETA ** (jnp.arange(0, ROPE_AXIS_DIM, 2, dtype=F32) / ROPE_AXIS_DIM))
    ar = row[:, None] * inv_freq[None, :]
    ac = col[:, None] * inv_freq[None, :]
    cos = jnp.concatenate([jnp.cos(ar), jnp.cos(ar), jnp.cos(ac), jnp.cos(ac)], axis=-1)
    sin = jnp.concatenate([-jnp.sin(ar), jnp.sin(ar), -jnp.sin(ac), jnp.sin(ac)], axis=-1)
    reps = LANES // HEAD_DIM
    return jnp.tile(cos, (1, reps)), jnp.tile(sin, (1, reps))


def _unit_circle(prod, length):
    ang = (prod % length).astype(F32) * (2.0 * math.pi / length)
    return jnp.cos(ang), jnp.sin(ang)


def _seq_dft_tables(n):
    length = n + N_META
    pos = jnp.concatenate([jnp.arange(n, dtype=jnp.int32) + N_META, jnp.arange(N_META, dtype=jnp.int32)])
    coarse = (jnp.arange(n // GRID_W, dtype=jnp.int32) * GRID_W)[:, None] * pos[None, :]
    fine = (jnp.arange(GRID_W, dtype=jnp.int32) + N_META)[:, None] * pos[None, :]
    c1, s1 = _unit_circle(coarse, length)
    c2, s2 = _unit_circle(fine, length)
    cos = (c1[:, None, :] * c2[None, :, :] - s1[:, None, :] * s2[None, :, :]).reshape(n, length).astype(BF16)
    sin = (s1[:, None, :] * c2[None, :, :] + c1[:, None, :] * s2[None, :, :]).reshape(n, length).astype(BF16)
    return cos[:, :n], sin[:, :n], cos[:, n:], sin[:, n:]


def _channel_dft_table(length):
    idx = jnp.arange(FOURIER_GROUP, dtype=jnp.int32)
    c, s = _unit_circle(idx[:, None] * idx[None, :], FOURIER_GROUP)
    scale = 1.0 / math.sqrt(length * FOURIER_GROUP)
    return (jnp.concatenate([c, -s], axis=1) * scale).astype(BF16)


def _encode(x, meta_tokens, p):
    b, n, d = x.shape
    length = n + N_META
    tok = jnp.arange(n, dtype=jnp.int32)
    cos, sin = _rope_tables((tok // GRID_W).astype(F32), (tok % GRID_W).astype(F32))
    meta_pad = LANES
    mrow = jnp.full((meta_pad,), -1.0, F32)
    mcol = jnp.arange(meta_pad, dtype=F32)
    cos_m, sin_m = _rope_tables(mrow, mcol)
    wc = _channel_dft_table(length)
    x_meta = jnp.zeros((1, meta_pad, d), F32).at[0, :N_META].set(meta_tokens)

    proj_consts = (p["norm_mix_g"], p["w_in"])
    head_consts = (p["gq"], p["gk"], p["kbias"], p["hmean"], wc)
    _, k_m, vt_m, z_m, _ = _in_proj(x_meta, *proj_consts, cos_m, sin_m, *head_consts, tm=meta_pad)
    qt, k, vt, z, gates = _in_proj(x, *proj_consts, cos, sin, *head_consts, tm=min(n, 512))

    attn = functools.partial(_attention, qt, k, vt, k_m[:, :, :N_META], vt_m[:, :, :N_META],
                             tq=min(n, 256), tk=min(n, 256))
    ot = lax.cond(p["bounded"], functools.partial(attn, bounded=True), functools.partial(attn, bounded=False))

    mc, ms, cm, sm = _seq_dft_tables(n)
    four = _seq_dft(z, z_m[:, :N_META], mc, ms, cm, sm, tm=min(n, 2048), tk=min(n, 1024))

    return _mix_ffn(x, ot, four, gates, p["w_attn_o"], p["w_four_o"], p["w_out"], p["norm_ffn_g"],
                    p["w_ffn_in"], p["w_ffn_out"], p["final_norm_g"], tm=min(n, 256), ff_chunk=1408)


def kernel(x_prompt, x_sample, meta_tokens, norm_mix_g, w_in, q_norm_g, k_norm_g, w_attn_o, w_four_o, w_out,
           norm_ffn_g, w_ffn_in, w_ffn_out, final_norm_g):
    assert w_in.shape[0] == 1, "single-layer encoder: meta rows are only needed as keys / DFT inputs"
    reps = LANES // HEAD_DIM
    head_id = jnp.arange(LANES) // HEAD_DIM
    bound = 1.01 * LOG2E * HEAD_DIM ** 0.5 * jnp.max(jnp.abs(q_norm_g[0])) * jnp.max(jnp.abs(k_norm_g[0]))
    bounded = bound <= MAX_SCORE_BOUND_LOG2
    p = {
        "bounded": bounded,
        "kbias": jnp.where((jnp.arange(LANES) == HEAD_DIM) & bounded, -bound, 0.0).astype(F32)[None, :],
        "norm_mix_g": norm_mix_g[0][None, :],
        "w_in": w_in[0].astype(BF16),
        "gq": jnp.tile(q_norm_g[0], reps)[None, :],
        "gk": jnp.tile(k_norm_g[0], reps)[None, :],
        "hmean": ((head_id[:, None] == head_id[None, :]).astype(F32) / HEAD_DIM).astype(BF16),
        "w_attn_o": w_attn_o[0].astype(BF16),
        "w_four_o": w_four_o[0].astype(BF16),
        "w_out": w_out[0].astype(BF16),
        "norm_ffn_g": norm_ffn_g[0][None, :],
        "w_ffn_in": w_ffn_in[0].astype(BF16),
        "w_ffn_out": w_ffn_out[0].astype(BF16),
        "final_norm_g": final_norm_g[None, :],
    }
    return (_encode(x_prompt, meta_tokens, p), _encode(x_sample, meta_tokens, p))
```

```python
import functools
import math

import jax
import jax.numpy as jnp
from jax import lax
from jax.experimental import pallas as pl
from jax.experimental.pallas import tpu as pltpu

N_META = 16
GRID_W = 64
HEAD_DIM = 64
N_Q_HEADS = 8
N_KV_HEADS = 2
Q_PER_KV = N_Q_HEADS // N_KV_HEADS
ATTN_WIDTH = N_Q_HEADS * HEAD_DIM
KV_WIDTH = N_KV_HEADS * HEAD_DIM
FOURIER_WIDTH = 512
FOURIER_GROUP = 128
N_FOURIER_GROUPS = FOURIER_WIDTH // FOURIER_GROUP
ROPE_AXIS_DIM = HEAD_DIM // 2
ROPE_THETA = 10000.0
EPS = 1e-6

LANES = 128
VMEM_LIMIT_BYTES = 56 * 1024 * 1024

F32 = jnp.float32
BF16 = jnp.bfloat16

LOG2E = math.log2(math.e)
Q_SCALE = HEAD_DIM ** -0.5 * LOG2E
MAX_SCORE_BOUND_LOG2 = 60.0
BOUNDED_Q_SUBTILE = 2 * LANES


def _dot(a, b):
    return jnp.dot(a, b, preferred_element_type=F32)


def _const_spec(shape):
    zeros = (0,) * len(shape)
    return pl.BlockSpec(shape, lambda *_: zeros, pipeline_mode=pl.Buffered(1))


def _params(n_axes):
    return pltpu.CompilerParams(dimension_semantics=("arbitrary",) * n_axes,
                                vmem_limit_bytes=VMEM_LIMIT_BYTES)


def _in_proj_kernel(x_ref, g_ref, w_ref, cos_ref, sin_ref, gq_ref, gk_ref, kbias_ref, hmean_ref, wc_ref,
                    qt_ref, k_ref, vt_ref, z_ref, gate_ref):
    x = x_ref[0]
    h = (x * lax.rsqrt(jnp.mean(x * x, axis=-1, keepdims=True) + EPS) * g_ref[...]).astype(BF16)
    cos = cos_ref[...]
    sin = sin_ref[...]
    lane = lax.broadcasted_iota(jnp.int32, cos.shape, 1)
    first_half = (lane % ROPE_AXIS_DIM) < (ROPE_AXIS_DIM // 2)

    def head_norm(z2):
        return z2 * lax.rsqrt(_dot((z2 * z2).astype(BF16), hmean_ref[...]) + EPS)

    def rope(y):
        half = ROPE_AXIS_DIM // 2
        partner = jnp.where(first_half, pltpu.roll(y, LANES - half, 1), pltpu.roll(y, half, 1))
        return y * cos + partner * sin

    o0 = ATTN_WIDTH
    o1 = o0 + KV_WIDTH
    o2 = o1 + KV_WIDTH
    o3 = o2 + FOURIER_WIDTH
    d_model = x.shape[-1]
    tile = 2 * LANES

    zq = _dot(h, w_ref[:, :o0])
    for c in range(ATTN_WIDTH // tile):
        yn = head_norm(zq[:, c * tile:(c + 1) * tile])
        for cc in range(tile // LANES):
            yq = rope(yn[:, cc * LANES:(cc + 1) * LANES] * gq_ref[...]) * Q_SCALE
            r0 = c * tile + cc * LANES
            qt_ref[0, r0:r0 + LANES, :] = yq.T.astype(BF16)

    zkv = _dot(h, w_ref[:, o0:o2])
    yk = rope(head_norm(zkv)[:, :KV_WIDTH] * gk_ref[...])
    low = lane < HEAD_DIM
    kbias = kbias_ref[...]
    k_ref[0, 0] = jnp.where(low, yk, kbias).astype(BF16)
    k_ref[0, 1] = jnp.where(low, pltpu.roll(yk, HEAD_DIM, 1), kbias).astype(BF16)

    vt_ref[0] = zkv[:, KV_WIDTH:].T.astype(BF16)

    zf = _dot(h, w_ref[:, o2:o3]).astype(BF16)
    for g in range(N_FOURIER_GROUPS):
        f = zf[:, g * FOURIER_GROUP:(g + 1) * FOURIER_GROUP]
        zc = _dot(f, wc_ref[...])
        z_ref[0, :, g * FOURIER_GROUP:(g + 1) * FOURIER_GROUP] = zc[:, :FOURIER_GROUP].astype(BF16)
        z_ref[0, :, FOURIER_WIDTH + g * FOURIER_GROUP:FOURIER_WIDTH + (g + 1) * FOURIER_GROUP] = (
            zc[:, FOURIER_GROUP:].astype(BF16))

    gate_chunk = 512
    for c in range(2 * d_model // gate_chunk):
        zg = _dot(h, w_ref[:, o3 + c * gate_chunk:o3 + (c + 1) * gate_chunk])
        gate_ref[0, :, c * gate_chunk:(c + 1) * gate_chunk] = (1.0 / (1.0 + jnp.exp(-zg))).astype(BF16)


def _in_proj(x, norm_g, w_in, cos, sin, gq, gk, kbias, hmean, wc, tm):
    b, n, d = x.shape
    in_width = w_in.shape[1]
    grid = (b, n // tm)
    out_shape = (
        jax.ShapeDtypeStruct((b, ATTN_WIDTH, n), BF16),
        jax.ShapeDtypeStruct((b, N_KV_HEADS, n, LANES), BF16),
        jax.ShapeDtypeStruct((b, KV_WIDTH, n), BF16),
        jax.ShapeDtypeStruct((b, n, 2 * FOURIER_WIDTH), BF16),
        jax.ShapeDtypeStruct((b, n, 2 * d), BF16),
    )
    return pl.pallas_call(
        _in_proj_kernel,
        grid=grid,
        in_specs=[
            pl.BlockSpec((1, tm, d), lambda i, j: (i, j, 0)),
            _const_spec((1, d)),
            _const_spec((d, in_width)),
            pl.BlockSpec((tm, LANES), lambda i, j: (j, 0)),
            pl.BlockSpec((tm, LANES), lambda i, j: (j, 0)),
            _const_spec((1, LANES)),
            _const_spec((1, LANES)),
            _const_spec((1, LANES)),
            _const_spec((2 * LANES, 2 * LANES)),
            _const_spec((FOURIER_GROUP, 2 * FOURIER_GROUP)),
        ],
        out_specs=(
            pl.BlockSpec((1, ATTN_WIDTH, tm), lambda i, j: (i, 0, j)),
            pl.BlockSpec((1, N_KV_HEADS, tm, LANES), lambda i, j: (i, 0, j, 0)),
            pl.BlockSpec((1, KV_WIDTH, tm), lambda i, j: (i, 0, j)),
            pl.BlockSpec((1, tm, 2 * FOURIER_WIDTH), lambda i, j: (i, j, 0)),
            pl.BlockSpec((1, tm, 2 * d), lambda i, j: (i, j, 0)),
        ),
        out_shape=out_shape,
        compiler_params=_params(2),
        name="in_proj",
    )(x, norm_g, w_in, cos, sin, gq, gk, kbias, hmean, wc)


def _attn_kernel(qt_ref, k_ref, vt_ref, km_ref, vtm_ref, ot_ref, *, tk, bounded):
    tq = qt_ref.shape[2]
    n_keys = k_ref.shape[2]
    n_blocks = n_keys // tk
    heads = range(Q_PER_KV)

    def head_qt(h, lo, width):
        ones_row = (lax.broadcasted_iota(jnp.int32, (HEAD_DIM, width), 0) == 0).astype(BF16)
        return jnp.concatenate([qt_ref[0, h * HEAD_DIM:(h + 1) * HEAD_DIM, lo:lo + width], ones_row], axis=0)

    if bounded:
        sub = min(tq, BOUNDED_Q_SUBTILE)
        for lo in range(0, tq, sub):
            qts = [head_qt(h, lo, sub) for h in heads]
            p = [jnp.exp2(_dot(km_ref[0, 0], qts[h])) for h in heads]
            l = [jnp.sum(p[h], axis=0, keepdims=True) for h in heads]
            acc = [_dot(vtm_ref[0], p[h].astype(BF16)) for h in heads]
            s = [_dot(k_ref[0, 0, 0:tk, :], qts[h]) for h in heads]
            for j in range(n_blocks):
                for h in heads:
                    ph = jnp.exp2(s[h])
                    l[h] = l[h] + jnp.sum(ph, axis=0, keepdims=True)
                    if j + 1 < n_blocks:
                        s[h] = _dot(k_ref[0, 0, (j + 1) * tk:(j + 2) * tk, :], qts[h])
                    acc[h] = acc[h] + _dot(vt_ref[0, :, j * tk:(j + 1) * tk], ph.astype(BF16))
            for h in heads:
                ot_ref[0, h * HEAD_DIM:(h + 1) * HEAD_DIM, lo:lo + sub] = (acc[h] / l[h]).astype(BF16)
        return

    qt = jnp.concatenate([head_qt(h, 0, tq) for h in heads], axis=1)

    def block(kb, vb, m, l, acc):
        s = _dot(kb, qt)
        m_new = jnp.maximum(m, jnp.max(s, axis=0, keepdims=True))
        alpha = jnp.exp2(m - m_new)
        p = jnp.exp2(s - m_new)
        return (m_new, alpha * l + jnp.sum(p, axis=0, keepdims=True), alpha * acc + _dot(vb, p.astype(BF16)))

    width = qt.shape[1]
    init = (jnp.full((1, width), -jnp.inf, F32), jnp.zeros((1, width), F32), jnp.zeros((HEAD_DIM, width), F32))
    carry = block(km_ref[0, 0], vtm_ref[0], *init)

    def body(j, carry):
        start = pl.multiple_of(j * tk, tk)
        return block(k_ref[0, 0, pl.ds(start, tk), :], vt_ref[0, :, pl.ds(start, tk)], *carry)

    _, l, acc = lax.fori_loop(0, n_blocks, body, carry)
    out = (acc / l).astype(BF16)
    for h in range(Q_PER_KV):
        ot_ref[0, h * HEAD_DIM:(h + 1) * HEAD_DIM, :] = out[:, h * tq:(h + 1) * tq]


def _attention(qt, k, vt, k_meta, vt_meta, tq, tk, bounded):
    b, _, n = qt.shape
    group_rows = Q_PER_KV * HEAD_DIM
    return pl.pallas_call(
        functools.partial(_attn_kernel, tk=tk, bounded=bounded),
        grid=(b, N_KV_HEADS, n // tq),
        in_specs=[
            pl.BlockSpec((1, group_rows, tq), lambda i, g, j: (i, g, j)),
            pl.BlockSpec((1, 1, n, LANES), lambda i, g, j: (i, g, 0, 0)),
            pl.BlockSpec((1, HEAD_DIM, n), lambda i, g, j: (i, g, 0)),
            pl.BlockSpec((1, 1, N_META, LANES), lambda i, g, j: (0, g, 0, 0)),
            pl.BlockSpec((1, HEAD_DIM, N_META), lambda i, g, j: (0, g, 0)),
        ],
        out_specs=pl.BlockSpec((1, group_rows, tq), lambda i, g, j: (i, g, j)),
        out_shape=jax.ShapeDtypeStruct((b, ATTN_WIDTH, n), BF16),
        compiler_params=_params(3),
        name="attention",
    )(qt, k, vt, k_meta, vt_meta)


def _seq_dft_kernel(mc_ref, ms_ref, z_ref, cm_ref, sm_ref, zm_ref, out_ref, acc_ref):
    kk = pl.program_id(2)

    @pl.when(kk == 0)
    def _():
        zm = zm_ref[0]
        acc_ref[...] = (_dot(cm_ref[...], zm[:, :FOURIER_WIDTH]) + _dot(sm_ref[...], zm[:, FOURIER_WIDTH:]))

    z = z_ref[0]
    acc_ref[...] += _dot(mc_ref[...], z[:, :FOURIER_WIDTH]) + _dot(ms_ref[...], z[:, FOURIER_WIDTH:])

    @pl.when(kk == pl.num_programs(2) - 1)
    def _():
        out_ref[0] = acc_ref[...].astype(out_ref.dtype)


def _seq_dft(z, z_meta, mc, ms, cm, sm, tm, tk):
    b, n, _ = z.shape
    return pl.pallas_call(
        _seq_dft_kernel,
        grid=(n // tm, b, n // tk),
        in_specs=[
            pl.BlockSpec((tm, tk), lambda r, i, kk: (r, kk)),
            pl.BlockSpec((tm, tk), lambda r, i, kk: (r, kk)),
            pl.BlockSpec((1, tk, 2 * FOURIER_WIDTH), lambda r, i, kk: (i, kk, 0)),
            pl.BlockSpec((tm, N_META), lambda r, i, kk: (r, 0)),
            pl.BlockSpec((tm, N_META), lambda r, i, kk: (r, 0)),
            pl.BlockSpec((1, N_META, 2 * FOURIER_WIDTH), lambda r, i, kk: (0, 0, 0)),
        ],
        out_specs=pl.BlockSpec((1, tm, FOURIER_WIDTH), lambda r, i, kk: (i, r, 0)),
        out_shape=jax.ShapeDtypeStruct((b, n, FOURIER_WIDTH), BF16),
        scratch_shapes=[pltpu.VMEM((tm, FOURIER_WIDTH), F32)],
        compiler_params=_params(3),
        name="seq_dft",
    )(mc, ms, z, cm, sm, z_meta)


def _rms(x, g):
    return x * lax.rsqrt(jnp.mean(x * x, axis=-1, keepdims=True) + EPS) * g


def _mix_ffn_kernel(x_ref, ot_ref, four_ref, gate_ref, wao_ref, wfo_ref, wout_ref, g2_ref, wfi_ref, wfo2_ref,
                    gfin_ref, y_ref, *, ff_chunk):
    x = x_ref[0]
    d_model = x.shape[-1]
    d_ff = wfo2_ref.shape[0]
    attn = ot_ref[0].astype(F32).T.astype(BF16)
    a = _dot(attn, wao_ref[...])
    fo = _dot(four_ref[0], wfo_ref[...])
    gates = gate_ref[0]
    merged = gates[:, :d_model].astype(F32) * a + gates[:, d_model:].astype(F32) * fo
    x1 = x + _dot(merged.astype(BF16), wout_ref[...])
    h2 = _rms(x1, g2_ref[...]).astype(BF16)
    acc = x1
    for c in range(d_ff // ff_chunk):
        gt = _dot(h2, wfi_ref[:, c * ff_chunk:(c + 1) * ff_chunk])
        up = _dot(h2, wfi_ref[:, d_ff + c * ff_chunk:d_ff + (c + 1) * ff_chunk])
        act = gt * (1.0 / (1.0 + jnp.exp(-gt))) * up
        acc = acc + _dot(act.astype(BF16), wfo2_ref[c * ff_chunk:(c + 1) * ff_chunk, :])
    y_ref[0] = _rms(acc, gfin_ref[...])


def _mix_ffn(x, ot, four, gates, w_attn_o, w_four_o, w_out, g2, w_ffn_in, w_ffn_out, gfin, tm, ff_chunk):
    b, n, d = x.shape
    d_ff = w_ffn_out.shape[0]
    return pl.pallas_call(
        functools.partial(_mix_ffn_kernel, ff_chunk=ff_chunk),
        grid=(b, n // tm),
        in_specs=[
            pl.BlockSpec((1, tm, d), lambda i, j: (i, j, 0)),
            pl.BlockSpec((1, ATTN_WIDTH, tm), lambda i, j: (i, 0, j)),
            pl.BlockSpec((1, tm, FOURIER_WIDTH), lambda i, j: (i, j, 0)),
            pl.BlockSpec((1, tm, 2 * d), lambda i, j: (i, j, 0)),
            _const_spec((ATTN_WIDTH, d)),
            _const_spec((FOURIER_WIDTH, d)),
            _const_spec((d, d)),
            _const_spec((1, d)),
            _const_spec((d, 2 * d_ff)),
            _const_spec((d_ff, d)),
            _const_spec((1, d)),
        ],
        out_specs=pl.BlockSpec((1, tm, d), lambda i, j: (i, j, 0)),
        out_shape=jax.ShapeDtypeStruct((b, n, d), F32),
        compiler_params=_params(2),
        name="mix_ffn",
    )(x, ot, four, gates, w_attn_o, w_four_o, w_out, g2, w_ffn_in, w_ffn_out, gfin)


def _rope_tables(row, col):
    inv_freq = 1.0 / (ROPE_THETA ** (jnp.arange(0, ROPE_AXIS_DIM, 2, dtype=F32) / ROPE_AXIS_DIM))
    ar = row[:, None] * inv_freq[None, :]
    ac = col[:, None] * inv_freq[None, :]
    cos = jnp.concatenate([jnp.cos(ar), jnp.cos(ar), jnp.cos(ac), jnp.cos(ac)], axis=-1)
    sin = jnp.concatenate([-jnp.sin(ar), jnp.sin(ar), -jnp.sin(ac), jnp.sin(ac)], axis=-1)
    reps = LANES // HEAD_DIM
    return jnp.tile(cos, (1, reps)), jnp.tile(sin, (1, reps))


def _unit_circle(prod, length):
    ang = (prod % length).astype(F32) * (2.0 * math.pi / length)
    return jnp.cos(ang), jnp.sin(ang)


def _seq_dft_table(n, pos):
    length = n + N_META
    coarse = (jnp.arange(n // GRID_W, dtype=jnp.int32) * GRID_W)[:, None] * pos[None, :]
    fine = (jnp.arange(GRID_W, dtype=jnp.int32) + N_META)[:, None] * pos[None, :]
    c1, s1 = _unit_circle(coarse, length)
    c2, s2 = _unit_circle(fine, length)
    cos = (c1[:, None, :] * c2[None, :, :] - s1[:, None, :] * s2[None, :, :]).reshape(n, pos.shape[0])
    sin = (s1[:, None, :] * c2[None, :, :] + c1[:, None, :] * s2[None, :, :]).reshape(n, pos.shape[0])
    return cos.astype(BF16), sin.astype(BF16)


def _seq_dft_tables(n):
    mc, ms = _seq_dft_table(n, jnp.arange(n, dtype=jnp.int32) + N_META)
    cm, sm = _seq_dft_table(n, jnp.arange(N_META, dtype=jnp.int32))
    return mc, ms, cm, sm


def _channel_dft_table(length):
    idx = jnp.arange(FOURIER_GROUP, dtype=jnp.int32)
    c, s = _unit_circle(idx[:, None] * idx[None, :], FOURIER_GROUP)
    scale = 1.0 / math.sqrt(length * FOURIER_GROUP)
    return (jnp.concatenate([c, -s], axis=1) * scale).astype(BF16)


def _encode(x, meta_tokens, p):
    b, n, d = x.shape
    length = n + N_META
    tok = jnp.arange(n, dtype=jnp.int32)
    cos, sin = _rope_tables((tok // GRID_W).astype(F32), (tok % GRID_W).astype(F32))
    meta_pad = LANES
    mrow = jnp.full((meta_pad,), -1.0, F32)
    mcol = jnp.arange(meta_pad, dtype=F32)
    cos_m, sin_m = _rope_tables(mrow, mcol)
    wc = _channel_dft_table(length)
    x_meta = jnp.zeros((1, meta_pad, d), F32).at[0, :N_META].set(meta_tokens)

    proj_consts = (p["norm_mix_g"], p["w_in"])
    head_consts = (p["gq"], p["gk"], p["kbias"], p["hmean"], wc)
    _, k_m, vt_m, z_m, _ = _in_proj(x_meta, *proj_consts, cos_m, sin_m, *head_consts, tm=meta_pad)
    qt, k, vt, z, gates = _in_proj(x, *proj_consts, cos, sin, *head_consts, tm=min(n, 512))

    attn = functools.partial(_attention, qt, k, vt, k_m[:, :, :N_META], vt_m[:, :, :N_META])
    ot = lax.cond(p["bounded"],
                  functools.partial(attn, tq=min(n, 2 * BOUNDED_Q_SUBTILE), tk=min(n, 256), bounded=True),
                  functools.partial(attn, tq=min(n, 256), tk=min(n, 512), bounded=False))

    mc, ms, cm, sm = _seq_dft_tables(n)
    four = _seq_dft(z, z_m[:, :N_META], mc, ms, cm, sm, tm=min(n, 2048), tk=min(n, 1024))

    return _mix_ffn(x, ot, four, gates, p["w_attn_o"], p["w_four_o"], p["w_out"], p["norm_ffn_g"],
                    p["w_ffn_in"], p["w_ffn_out"], p["final_norm_g"], tm=min(n, 256),
                    ff_chunk=p["w_ffn_out"].shape[0])


def kernel(x_prompt, x_sample, meta_tokens, norm_mix_g, w_in, q_norm_g, k_norm_g, w_attn_o, w_four_o, w_out,
           norm_ffn_g, w_ffn_in, w_ffn_out, final_norm_g):
    assert w_in.shape[0] == 1, "single-layer encoder: meta rows are only needed as keys / DFT inputs"
    reps = LANES // HEAD_DIM
    head_id = jnp.arange(2 * LANES) // HEAD_DIM
    bound = 1.01 * LOG2E * HEAD_DIM ** 0.5 * jnp.max(jnp.abs(q_norm_g[0])) * jnp.max(jnp.abs(k_norm_g[0]))
    bounded = bound <= MAX_SCORE_BOUND_LOG2
    p = {
        "bounded": bounded,
        "kbias": jnp.where((jnp.arange(LANES) == HEAD_DIM) & bounded, -bound, 0.0).astype(F32)[None, :],
        "norm_mix_g": norm_mix_g[0][None, :],
        "w_in": w_in[0].astype(BF16),
        "gq": jnp.tile(q_norm_g[0], reps)[None, :],
        "gk": jnp.tile(k_norm_g[0], reps)[None, :],
        "hmean": ((head_id[:, None] == head_id[None, :]).astype(F32) / HEAD_DIM).astype(BF16),
        "w_attn_o": w_attn_o[0].astype(BF16),
        "w_four_o": w_four_o[0].astype(BF16),
        "w_out": w_out[0].astype(BF16),
        "norm_ffn_g": norm_ffn_g[0][None, :],
        "w_ffn_in": w_ffn_in[0].astype(BF16),
        "w_ffn_out": w_ffn_out[0].astype(BF16),
        "final_norm_g": final_norm_g[None, :],
    }
    return (_encode(x_prompt, meta_tokens, p), _encode(x_sample, meta_tokens, p))
```

```python
import functools
import math

import jax
import jax.numpy as jnp
from jax import lax
from jax.experimental import pallas as pl
from jax.experimental.pallas import tpu as pltpu

N_META = 16
GRID_W = 64
HEAD_DIM = 64
N_Q_HEADS = 8
N_KV_HEADS = 2
Q_PER_KV = N_Q_HEADS // N_KV_HEADS
ATTN_WIDTH = N_Q_HEADS * HEAD_DIM
KV_WIDTH = N_KV_HEADS * HEAD_DIM
FOURIER_WIDTH = 512
FOURIER_GROUP = 128
N_FOURIER_GROUPS = FOURIER_WIDTH // FOURIER_GROUP
ROPE_AXIS_DIM = HEAD_DIM // 2
ROPE_THETA = 10000.0
EPS = 1e-6

LANES = 128
VMEM_LIMIT_BYTES = 56 * 1024 * 1024

F32 = jnp.float32
BF16 = jnp.bfloat16

LOG2E = math.log2(math.e)
Q_SCALE = HEAD_DIM ** -0.5 * LOG2E
MAX_SCORE_BOUND_LOG2 = 60.0
BOUNDED_Q_SUBTILE = 2 * LANES


def _dot(a, b):
    return jnp.dot(a, b, preferred_element_type=F32)


def _const_spec(shape):
    zeros = (0,) * len(shape)
    return pl.BlockSpec(shape, lambda *_: zeros, pipeline_mode=pl.Buffered(1))


def _params(n_axes):
    return pltpu.CompilerParams(dimension_semantics=("arbitrary",) * n_axes,
                                vmem_limit_bytes=VMEM_LIMIT_BYTES)


def _in_proj_kernel(x_ref, g_ref, w_ref, cos_ref, sin_ref, gq_ref, gk_ref, kbias_ref, hmean_ref, wc_ref,
                    qt_ref, k_ref, vt_ref, z_ref, gate_ref):
    x = x_ref[0]
    h = (x * lax.rsqrt(jnp.mean(x * x, axis=-1, keepdims=True) + EPS) * g_ref[...]).astype(BF16)
    cos = cos_ref[...]
    sin = sin_ref[...]
    lane = lax.broadcasted_iota(jnp.int32, cos.shape, 1)
    first_half = (lane % ROPE_AXIS_DIM) < (ROPE_AXIS_DIM // 2)

    def head_norm(z2):
        return z2 * lax.rsqrt(_dot((z2 * z2).astype(BF16), hmean_ref[...]) + EPS)

    def rope(y):
        half = ROPE_AXIS_DIM // 2
        partner = jnp.where(first_half, pltpu.roll(y, LANES - half, 1), pltpu.roll(y, half, 1))
        return y * cos + partner * sin

    o0 = ATTN_WIDTH
    o1 = o0 + KV_WIDTH
    o2 = o1 + KV_WIDTH
    o3 = o2 + FOURIER_WIDTH
    d_model = x.shape[-1]
    tile = 2 * LANES

    zq = _dot(h, w_ref[:, :o0])
    for c in range(ATTN_WIDTH // tile):
        yn = head_norm(zq[:, c * tile:(c + 1) * tile])
        for cc in range(tile // LANES):
            yq = rope(yn[:, cc * LANES:(cc + 1) * LANES] * gq_ref[...]) * Q_SCALE
            r0 = c * tile + cc * LANES
            qt_ref[0, r0:r0 + LANES, :] = yq.T.astype(BF16)

    zkv = _dot(h, w_ref[:, o0:o2])
    yk = rope(head_norm(zkv)[:, :KV_WIDTH] * gk_ref[...])
    low = lane < HEAD_DIM
    kbias = kbias_ref[...]
    k_ref[0, 0] = jnp.where(low, yk, kbias).astype(BF16)
    k_ref[0, 1] = jnp.where(low, pltpu.roll(yk, HEAD_DIM, 1), kbias).astype(BF16)

    vt_ref[0] = zkv[:, KV_WIDTH:].T.astype(BF16)

    zf = _dot(h, w_ref[:, o2:o3]).astype(BF16)
    if z_ref.shape[-1] == FOURIER_WIDTH:
        z_ref[0] = zf
    else:
        for g in range(N_FOURIER_GROUPS):
            f = zf[:, g * FOURIER_GROUP:(g + 1) * FOURIER_GROUP]
            zc = _dot(f, wc_ref[...])
            z_ref[0, :, g * FOURIER_GROUP:(g + 1) * FOURIER_GROUP] = zc[:, :FOURIER_GROUP].astype(BF16)
            z_ref[0, :, FOURIER_WIDTH + g * FOURIER_GROUP:FOURIER_WIDTH + (g + 1) * FOURIER_GROUP] = (
                zc[:, FOURIER_GROUP:].astype(BF16))

    gate_chunk = 512
    for c in range(2 * d_model // gate_chunk):
        zg = _dot(h, w_ref[:, o3 + c * gate_chunk:o3 + (c + 1) * gate_chunk])
        gate_ref[0, :, c * gate_chunk:(c + 1) * gate_chunk] = (1.0 / (1.0 + jnp.exp(-zg))).astype(BF16)


def _in_proj(x, norm_g, w_in, cos, sin, gq, gk, kbias, hmean, wc, tm, channel_dft):
    b, n, d = x.shape
    in_width = w_in.shape[1]
    grid = (b, n // tm)
    four_width = 2 * FOURIER_WIDTH if channel_dft else FOURIER_WIDTH
    out_shape = (
        jax.ShapeDtypeStruct((b, ATTN_WIDTH, n), BF16),
        jax.ShapeDtypeStruct((b, N_KV_HEADS, n, LANES), BF16),
        jax.ShapeDtypeStruct((b, KV_WIDTH, n), BF16),
        jax.ShapeDtypeStruct((b, n, four_width), BF16),
        jax.ShapeDtypeStruct((b, n, 2 * d), BF16),
    )
    return pl.pallas_call(
        _in_proj_kernel,
        grid=grid,
        in_specs=[
            pl.BlockSpec((1, tm, d), lambda i, j: (i, j, 0)),
            _const_spec((1, d)),
            _const_spec((d, in_width)),
            pl.BlockSpec((tm, LANES), lambda i, j: (j, 0)),
            pl.BlockSpec((tm, LANES), lambda i, j: (j, 0)),
            _const_spec((1, LANES)),
            _const_spec((1, LANES)),
            _const_spec((1, LANES)),
            _const_spec((2 * LANES, 2 * LANES)),
            _const_spec((FOURIER_GROUP, 2 * FOURIER_GROUP)),
        ],
        out_specs=(
            pl.BlockSpec((1, ATTN_WIDTH, tm), lambda i, j: (i, 0, j)),
            pl.BlockSpec((1, N_KV_HEADS, tm, LANES), lambda i, j: (i, 0, j, 0)),
            pl.BlockSpec((1, KV_WIDTH, tm), lambda i, j: (i, 0, j)),
            pl.BlockSpec((1, tm, four_width), lambda i, j: (i, j, 0)),
            pl.BlockSpec((1, tm, 2 * d), lambda i, j: (i, j, 0)),
        ),
        out_shape=out_shape,
        compiler_params=_params(2),
        name="in_proj",
    )(x, norm_g, w_in, cos, sin, gq, gk, kbias, hmean, wc)


def _attn_kernel(qt_ref, k_ref, vt_ref, km_ref, vtm_ref, ot_ref, *, tk, bounded):
    tq = qt_ref.shape[2]
    n_keys = k_ref.shape[2]
    n_blocks = n_keys // tk
    heads = range(Q_PER_KV)

    def head_qt(h, lo, width):
        ones_row = (lax.broadcasted_iota(jnp.int32, (HEAD_DIM, width), 0) == 0).astype(BF16)
        return jnp.concatenate([qt_ref[0, h * HEAD_DIM:(h + 1) * HEAD_DIM, lo:lo + width], ones_row], axis=0)

    if bounded:
        sub = min(tq, BOUNDED_Q_SUBTILE)
        for lo in range(0, tq, sub):
            qts = [head_qt(h, lo, sub) for h in heads]
            p = [jnp.exp2(_dot(km_ref[0, 0], qts[h])) for h in heads]
            l = [jnp.sum(p[h], axis=0, keepdims=True) for h in heads]
            acc = [_dot(vtm_ref[0], p[h].astype(BF16)) for h in heads]
            s = [_dot(k_ref[0, 0, 0:tk, :], qts[h]) for h in heads]
            for j in range(n_blocks):
                for h in heads:
                    ph = jnp.exp2(s[h])
                    l[h] = l[h] + jnp.sum(ph, axis=0, keepdims=True)
                    if j + 1 < n_blocks:
                        s[h] = _dot(k_ref[0, 0, (j + 1) * tk:(j + 2) * tk, :], qts[h])
                    acc[h] = acc[h] + _dot(vt_ref[0, :, j * tk:(j + 1) * tk], ph.astype(BF16))
            for h in heads:
                ot_ref[0, h * HEAD_DIM:(h + 1) * HEAD_DIM, lo:lo + sub] = (acc[h] / l[h]).astype(BF16)
        return

    qt = jnp.concatenate([head_qt(h, 0, tq) for h in heads], axis=1)

    def block(kb, vb, m, l, acc):
        s = _dot(kb, qt)
        m_new = jnp.maximum(m, jnp.max(s, axis=0, keepdims=True))
        alpha = jnp.exp2(m - m_new)
        p = jnp.exp2(s - m_new)
        return (m_new, alpha * l + jnp.sum(p, axis=0, keepdims=True), alpha * acc + _dot(vb, p.astype(BF16)))

    width = qt.shape[1]
    init = (jnp.full((1, width), -jnp.inf, F32), jnp.zeros((1, width), F32), jnp.zeros((HEAD_DIM, width), F32))
    carry = block(km_ref[0, 0], vtm_ref[0], *init)

    def body(j, carry):
        start = pl.multiple_of(j * tk, tk)
        return block(k_ref[0, 0, pl.ds(start, tk), :], vt_ref[0, :, pl.ds(start, tk)], *carry)

    _, l, acc = lax.fori_loop(0, n_blocks, body, carry)
    out = (acc / l).astype(BF16)
    for h in range(Q_PER_KV):
        ot_ref[0, h * HEAD_DIM:(h + 1) * HEAD_DIM, :] = out[:, h * tq:(h + 1) * tq]


def _attention(qt, k, vt, k_meta, vt_meta, tq, tk, bounded):
    b, _, n = qt.shape
    group_rows = Q_PER_KV * HEAD_DIM
    return pl.pallas_call(
        functools.partial(_attn_kernel, tk=tk, bounded=bounded),
        grid=(b, N_KV_HEADS, n // tq),
        in_specs=[
            pl.BlockSpec((1, group_rows, tq), lambda i, g, j: (i, g, j)),
            pl.BlockSpec((1, 1, n, LANES), lambda i, g, j: (i, g, 0, 0)),
            pl.BlockSpec((1, HEAD_DIM, n), lambda i, g, j: (i, g, 0)),
            pl.BlockSpec((1, 1, N_META, LANES), lambda i, g, j: (0, g, 0, 0)),
            pl.BlockSpec((1, HEAD_DIM, N_META), lambda i, g, j: (0, g, 0)),
        ],
        out_specs=pl.BlockSpec((1, group_rows, tq), lambda i, g, j: (i, g, j)),
        out_shape=jax.ShapeDtypeStruct((b, ATTN_WIDTH, n), BF16),
        compiler_params=_params(3),
        name="attention",
    )(qt, k, vt, k_meta, vt_meta)


def _seq_dft_kernel(mc_ref, ms_ref, z_ref, cm_ref, sm_ref, zm_ref, out_ref, acc_ref):
    kk = pl.program_id(2)

    @pl.when(kk == 0)
    def _():
        zm = zm_ref[0]
        acc_ref[...] = (_dot(cm_ref[...], zm[:, :FOURIER_WIDTH]) + _dot(sm_ref[...], zm[:, FOURIER_WIDTH:]))

    z = z_ref[0]
    acc_ref[...] += _dot(mc_ref[...], z[:, :FOURIER_WIDTH]) + _dot(ms_ref[...], z[:, FOURIER_WIDTH:])

    @pl.when(kk == pl.num_programs(2) - 1)
    def _():
        out_ref[0] = acc_ref[...].astype(out_ref.dtype)


def _seq_dft(z, z_meta, mc, ms, cm, sm, tm, tk):
    b, n, _ = z.shape
    return pl.pallas_call(
        _seq_dft_kernel,
        grid=(n // tm, b, n // tk),
        in_specs=[
            pl.BlockSpec((tm, tk), lambda r, i, kk: (r, kk)),
            pl.BlockSpec((tm, tk), lambda r, i, kk: (r, kk)),
            pl.BlockSpec((1, tk, 2 * FOURIER_WIDTH), lambda r, i, kk: (i, kk, 0)),
            pl.BlockSpec((tm, N_META), lambda r, i, kk: (r, 0)),
            pl.BlockSpec((tm, N_META), lambda r, i, kk: (r, 0)),
            pl.BlockSpec((1, N_META, 2 * FOURIER_WIDTH), lambda r, i, kk: (0, 0, 0)),
        ],
        out_specs=pl.BlockSpec((1, tm, FOURIER_WIDTH), lambda r, i, kk: (i, r, 0)),
        out_shape=jax.ShapeDtypeStruct((b, n, FOURIER_WIDTH), BF16),
        scratch_shapes=[pltpu.VMEM((tm, FOURIER_WIDTH), F32)],
        compiler_params=_params(3),
        name="seq_dft",
    )(mc, ms, z, cm, sm, z_meta)


FFT_RADIX = N_META
FFT_A_RESIDUES_PER_STEP = 4
FFT_MIN_TOKENS = 4096


def _fft_a_kernel(e_ref, x_ref, t_ref, xm_ref, out_ref, meta_ref):
    n1p = out_ref.shape[1]
    width = FOURIER_WIDTH

    @pl.when(pl.program_id(1) == 0)
    def _():
        for q in range(FFT_A_RESIDUES_PER_STEP):
            meta_ref[q] = _dot(t_ref[q], xm_ref[...])

    for q in range(FFT_A_RESIDUES_PER_STEP):
        r = _dot(e_ref[q], x_ref[0, :, q * width:(q + 1) * width]) + meta_ref[q]
        for g in range(N_FOURIER_GROUPS):
            base = (q * N_FOURIER_GROUPS + g) * 2 * FOURIER_GROUP
            cols = slice(g * FOURIER_GROUP, (g + 1) * FOURIER_GROUP)
            out_ref[0, :, base:base + FOURIER_GROUP] = r[:n1p, cols].astype(BF16)
            out_ref[0, :, base + FOURIER_GROUP:base + 2 * FOURIER_GROUP] = r[n1p:, cols].astype(BF16)


def _fft_b_kernel(ab_ref, mk_ref, out_ref):
    per_residue = N_FOURIER_GROUPS * 2 * FOURIER_GROUP
    for g in range(N_FOURIER_GROUPS):
        lo = g * 2 * FOURIER_GROUP
        lhs = jnp.concatenate([ab_ref[0, :, q * per_residue + lo:q * per_residue + lo + 2 * FOURIER_GROUP]
                               for q in range(FFT_RADIX)], axis=1)
        res = _dot(lhs, mk_ref[...])
        for k2 in range(FFT_RADIX):
            out_ref[0, k2, :, g * FOURIER_GROUP:(g + 1) * FOURIER_GROUP] = (
                res[:, k2 * FOURIER_GROUP:(k2 + 1) * FOURIER_GROUP].astype(BF16))


def _fft_row_tile(n1p):
    return max(t for t in range(16, 193, 16) if n1p % t == 0)


def _seq_fft(f, f_meta, e, t, mk):
    b, n, width = f.shape
    m = n // FFT_RADIX
    n1 = m + 1
    n1p = e.shape[1] // 2
    qs = FFT_A_RESIDUES_PER_STEP
    per_residue = N_FOURIER_GROUPS * 2 * FOURIER_GROUP
    xv = f.reshape(b, m, FFT_RADIX * width)
    ab = pl.pallas_call(
        _fft_a_kernel,
        grid=(FFT_RADIX // qs, b),
        in_specs=[
            pl.BlockSpec((qs, 2 * n1p, m), lambda q, i: (q, 0, 0)),
            pl.BlockSpec((1, m, qs * width), lambda q, i: (i, 0, q)),
            pl.BlockSpec((qs, 2 * n1p, FFT_RADIX), lambda q, i: (q, 0, 0)),
            pl.BlockSpec((FFT_RADIX, width), lambda q, i: (0, 0)),
        ],
        out_specs=pl.BlockSpec((1, n1p, qs * per_residue), lambda q, i: (i, 0, q)),
        out_shape=jax.ShapeDtypeStruct((b, n1p, FFT_RADIX * per_residue), BF16),
        scratch_shapes=[pltpu.VMEM((qs, 2 * n1p, width), F32)],
        compiler_params=_params(2),
        name="seq_fft_a",
    )(e, xv, t, f_meta)
    tr = _fft_row_tile(n1p)
    out = pl.pallas_call(
        _fft_b_kernel,
        grid=(b, n1p // tr),
        in_specs=[
            pl.BlockSpec((1, tr, FFT_RADIX * per_residue), lambda i, r: (i, r, 0)),
            _const_spec(mk.shape),
        ],
        out_specs=pl.BlockSpec((1, FFT_RADIX, tr, width), lambda i, r: (i, 0, r, 0)),
        out_shape=jax.ShapeDtypeStruct((b, FFT_RADIX, n1p, width), BF16),
        compiler_params=_params(2),
        name="seq_fft_b",
    )(ab, mk)
    return out[:, :, :n1].reshape(b, FFT_RADIX * n1, width)[:, N_META:]


def _seq_fft_tables(n):
    length = n + N_META
    m = n // FFT_RADIX
    n1 = m + 1
    n1p = -(-n1 // 16) * 16
    k1 = jnp.arange(n1p, dtype=jnp.int32)
    valid = (k1 < n1)[None, :, None]
    n2 = jnp.arange(FFT_RADIX, dtype=jnp.int32)
    pos = FFT_RADIX * (jnp.arange(m, dtype=jnp.int32) + 1)[None, :] + n2[:, None]
    c, s = _unit_circle(k1[None, :, None] * pos[:, None, :], length)
    e = jnp.concatenate([jnp.where(valid, c, 0.0), jnp.where(valid, -s, 0.0)], axis=1).astype(BF16)
    cm, sm = _unit_circle(k1[None, :] * n2[:, None], length)
    pick = (n2[:, None, None] == n2[None, None, :]) & valid
    t = jnp.concatenate([jnp.where(pick, cm[:, :, None], 0.0), jnp.where(pick, -sm[:, :, None], 0.0)],
                        axis=1).astype(BF16)
    ch = jnp.arange(FOURIER_GROUP, dtype=jnp.int32)
    turns = ((FOURIER_GROUP // FFT_RADIX) * n2[:, None, None, None] * n2[None, None, :, None]
             + ch[None, :, None, None] * ch[None, None, None, :])
    ck, sk = _unit_circle(turns, FOURIER_GROUP)
    scale = 1.0 / math.sqrt(length * FOURIER_GROUP)
    mk = (jnp.stack([ck, sk], axis=1) * scale).reshape(FFT_RADIX * 2 * FOURIER_GROUP, FFT_RADIX * FOURIER_GROUP)
    return e, t, mk.astype(BF16)


def _rms(x, g):
    return x * lax.rsqrt(jnp.mean(x * x, axis=-1, keepdims=True) + EPS) * g


def _mix_ffn_kernel(x_ref, ot_ref, four_ref, gate_ref, wao_ref, wfo_ref, wout_ref, g2_ref, wfi_ref, wfo2_ref,
                    gfin_ref, y_ref, *, ff_chunk):
    x = x_ref[0]
    d_model = x.shape[-1]
    d_ff = wfo2_ref.shape[0]
    attn = ot_ref[0].astype(F32).T.astype(BF16)
    a = _dot(attn, wao_ref[...])
    fo = _dot(four_ref[0], wfo_ref[...])
    gates = gate_ref[0]
    merged = gates[:, :d_model].astype(F32) * a + gates[:, d_model:].astype(F32) * fo
    x1 = x + _dot(merged.astype(BF16), wout_ref[...])
    h2 = _rms(x1, g2_ref[...]).astype(BF16)
    acc = x1
    for c in range(d_ff // ff_chunk):
        gt = _dot(h2, wfi_ref[:, c * ff_chunk:(c + 1) * ff_chunk])
        up = _dot(h2, wfi_ref[:, d_ff + c * ff_chunk:d_ff + (c + 1) * ff_chunk])
        act = gt * (1.0 / (1.0 + jnp.exp(-gt))) * up
        acc = acc + _dot(act.astype(BF16), wfo2_ref[c * ff_chunk:(c + 1) * ff_chunk, :])
    y_ref[0] = _rms(acc, gfin_ref[...])


def _mix_ffn(x, ot, four, gates, w_attn_o, w_four_o, w_out, g2, w_ffn_in, w_ffn_out, gfin, tm, ff_chunk):
    b, n, d = x.shape
    d_ff = w_ffn_out.shape[0]
    return pl.pallas_call(
        functools.partial(_mix_ffn_kernel, ff_chunk=ff_chunk),
        grid=(b, n // tm),
        in_specs=[
            pl.BlockSpec((1, tm, d), lambda i, j: (i, j, 0)),
            pl.BlockSpec((1, ATTN_WIDTH, tm), lambda i, j: (i, 0, j)),
            pl.BlockSpec((1, tm, FOURIER_WIDTH), lambda i, j: (i, j, 0)),
            pl.BlockSpec((1, tm, 2 * d), lambda i, j: (i, j, 0)),
            _const_spec((ATTN_WIDTH, d)),
            _const_spec((FOURIER_WIDTH, d)),
            _const_spec((d, d)),
            _const_spec((1, d)),
            _const_spec((d, 2 * d_ff)),
            _const_spec((d_ff, d)),
            _const_spec((1, d)),
        ],
        out_specs=pl.BlockSpec((1, tm, d), lambda i, j: (i, j, 0)),
        out_shape=jax.ShapeDtypeStruct((b, n, d), F32),
        compiler_params=_params(2),
        name="mix_ffn",
    )(x, ot, four, gates, w_attn_o, w_four_o, w_out, g2, w_ffn_in, w_ffn_out, gfin)


def _rope_tables(row, col):
    inv_freq = 1.0 / (ROPE_THETA ** (jnp.arange(0, ROPE_AXIS_DIM, 2, dtype=F32) / ROPE_AXIS_DIM))
    ar = row[:, None] * inv_freq[None, :]
    ac = col[:, None] * inv_freq[None, :]
    cos = jnp.concatenate([jnp.cos(ar), jnp.cos(ar), jnp.cos(ac), jnp.cos(ac)], axis=-1)
    sin = jnp.concatenate([-jnp.sin(ar), jnp.sin(ar), -jnp.sin(ac), jnp.sin(ac)], axis=-1)
    reps = LANES // HEAD_DIM
    return jnp.tile(cos, (1, reps)), jnp.tile(sin, (1, reps))


def _unit_circle(prod, length):
    ang = (prod % length).astype(F32) * (2.0 * math.pi / length)
    return jnp.cos(ang), jnp.sin(ang)


def _seq_dft_table(n, pos):
    length = n + N_META
    coarse = (jnp.arange(n // GRID_W, dtype=jnp.int32) * GRID_W)[:, None] * pos[None, :]
    fine = (jnp.arange(GRID_W, dtype=jnp.int32) + N_META)[:, None] * pos[None, :]
    c1, s1 = _unit_circle(coarse, length)
    c2, s2 = _unit_circle(fine, length)
    cos = (c1[:, None, :] * c2[None, :, :] - s1[:, None, :] * s2[None, :, :]).reshape(n, pos.shape[0])
    sin = (s1[:, None, :] * c2[None, :, :] + c1[:, None, :] * s2[None, :, :]).reshape(n, pos.shape[0])
    return cos.astype(BF16), sin.astype(BF16)


def _seq_dft_tables(n):
    mc, ms = _seq_dft_table(n, jnp.arange(n, dtype=jnp.int32) + N_META)
    cm, sm = _seq_dft_table(n, jnp.arange(N_META, dtype=jnp.int32))
    return mc, ms, cm, sm


def _channel_dft_table(length):
    idx = jnp.arange(FOURIER_GROUP, dtype=jnp.int32)
    c, s = _unit_circle(idx[:, None] * idx[None, :], FOURIER_GROUP)
    scale = 1.0 / math.sqrt(length * FOURIER_GROUP)
    return (jnp.concatenate([c, -s], axis=1) * scale).astype(BF16)


def _encode(x, meta_tokens, p):
    b, n, d = x.shape
    length = n + N_META
    tok = jnp.arange(n, dtype=jnp.int32)
    cos, sin = _rope_tables((tok // GRID_W).astype(F32), (tok % GRID_W).astype(F32))
    meta_pad = LANES
    mrow = jnp.full((meta_pad,), -1.0, F32)
    mcol = jnp.arange(meta_pad, dtype=F32)
    cos_m, sin_m = _rope_tables(mrow, mcol)
    wc = _channel_dft_table(length)
    x_meta = jnp.zeros((1, meta_pad, d), F32).at[0, :N_META].set(meta_tokens)

    dense_dft = n < FFT_MIN_TOKENS
    proj_consts = (p["norm_mix_g"], p["w_in"])
    head_consts = (p["gq"], p["gk"], p["kbias"], p["hmean"], wc)
    _, k_m, vt_m, z_m, _ = _in_proj(x_meta, *proj_consts, cos_m, sin_m, *head_consts, tm=meta_pad,
                                    channel_dft=dense_dft)
    qt, k, vt, z, gates = _in_proj(x, *proj_consts, cos, sin, *head_consts, tm=min(n, 512), channel_dft=dense_dft)

    attn = functools.partial(_attention, qt, k, vt, k_m[:, :, :N_META], vt_m[:, :, :N_META])
    ot = lax.cond(p["bounded"],
                  functools.partial(attn, tq=min(n, 2 * BOUNDED_Q_SUBTILE), tk=min(n, 256), bounded=True),
                  functools.partial(attn, tq=min(n, 256), tk=min(n, 512), bounded=False))

    if dense_dft:
        mc, ms, cm, sm = _seq_dft_tables(n)
        four = _seq_dft(z, z_m[:, :N_META], mc, ms, cm, sm, tm=min(n, 2048), tk=min(n, 1024))
    else:
        four = _seq_fft(z, z_m[0, :N_META], *_seq_fft_tables(n))

    return _mix_ffn(x, ot, four, gates, p["w_attn_o"], p["w_four_o"], p["w_out"], p["norm_ffn_g"],
                    p["w_ffn_in"], p["w_ffn_out"], p["final_norm_g"], tm=min(n, 256),
                    ff_chunk=p["w_ffn_out"].shape[0])


def kernel(x_prompt, x_sample, meta_tokens, norm_mix_g, w_in, q_norm_g, k_norm_g, w_attn_o, w_four_o, w_out,
           norm_ffn_g, w_ffn_in, w_ffn_out, final_norm_g):
    assert w_in.shape[0] == 1, "single-layer encoder: meta rows are only needed as keys / DFT inputs"
    reps = LANES // HEAD_DIM
    head_id = jnp.arange(2 * LANES) // HEAD_DIM
    bound = 1.01 * LOG2E * HEAD_DIM ** 0.5 * jnp.max(jnp.abs(q_norm_g[0])) * jnp.max(jnp.abs(k_norm_g[0]))
    bounded = bound <= MAX_SCORE_BOUND_LOG2
    p = {
        "bounded": bounded,
        "kbias": jnp.where((jnp.arange(LANES) == HEAD_DIM) & bounded, -bound, 0.0).astype(F32)[None, :],
        "norm_mix_g": norm_mix_g[0][None, :],
        "w_in": w_in[0].astype(BF16),
        "gq": jnp.tile(q_norm_g[0], reps)[None, :],
        "gk": jnp.tile(k_norm_g[0], reps)[None, :],
        "hmean": ((head_id[:, None] == head_id[None, :]).astype(F32) / HEAD_DIM).astype(BF16),
        "w_attn_o": w_attn_o[0].astype(BF16),
        "w_four_o": w_four_o[0].astype(BF16),
        "w_out": w_out[0].astype(BF16),
        "norm_ffn_g": norm_ffn_g[0][None, :],
        "w_ffn_in": w_ffn_in[0].astype(BF16),
        "w_ffn_out": w_ffn_out[0].astype(BF16),
        "final_norm_g": final_norm_g[None, :],
    }
    return (_encode(x_prompt, meta_tokens, p), _encode(x_sample, meta_tokens, p))
```

```python
import functools
import math

import jax
import jax.numpy as jnp
from jax import lax
from jax.experimental import pallas as pl
from jax.experimental.pallas import tpu as pltpu

N_META = 16
GRID_W = 64
HEAD_DIM = 64
N_Q_HEADS = 8
N_KV_HEADS = 2
Q_PER_KV = N_Q_HEADS // N_KV_HEADS
ATTN_WIDTH = N_Q_HEADS * HEAD_DIM
KV_WIDTH = N_KV_HEADS * HEAD_DIM
FOURIER_WIDTH = 512
FOURIER_GROUP = 128
N_FOURIER_GROUPS = FOURIER_WIDTH // FOURIER_GROUP
ROPE_AXIS_DIM = HEAD_DIM // 2
ROPE_THETA = 10000.0
EPS = 1e-6

LANES = 128
VMEM_LIMIT_BYTES = 56 * 1024 * 1024

F32 = jnp.float32
BF16 = jnp.bfloat16

LOG2E = math.log2(math.e)
Q_SCALE = HEAD_DIM ** -0.5 * LOG2E
MAX_SCORE_BOUND_LOG2 = 60.0
BOUNDED_Q_SUBTILE = 2 * LANES

FFT_RADIX = N_META
FFT_A_RESIDUES_PER_STEP = 4
FFT_MIN_TOKENS = 4096


def _dot(a, b):
    return jnp.dot(a, b, preferred_element_type=F32)


def _const_spec(shape):
    zeros = (0,) * len(shape)
    return pl.BlockSpec(shape, lambda *_: zeros, pipeline_mode=pl.Buffered(1))


def _params(n_axes):
    return pltpu.CompilerParams(dimension_semantics=("arbitrary",) * n_axes,
                                vmem_limit_bytes=VMEM_LIMIT_BYTES)


def _in_proj_kernel(x_ref, g_ref, w_ref, cos_ref, sin_ref, gq_ref, gk_ref, kbias_ref, hmean_ref, wc_ref,
                    qt_ref, k_ref, vt_ref, z_ref, gate_ref, *slab_ref):
    x = x_ref[0]
    h = (x * lax.rsqrt(jnp.mean(x * x, axis=-1, keepdims=True) + EPS) * g_ref[...]).astype(BF16)
    cos = cos_ref[...]
    sin = sin_ref[...]
    lane = lax.broadcasted_iota(jnp.int32, cos.shape, 1)
    first_half = (lane % ROPE_AXIS_DIM) < (ROPE_AXIS_DIM // 2)

    def head_norm(z2):
        return z2 * lax.rsqrt(_dot((z2 * z2).astype(BF16), hmean_ref[...]) + EPS)

    def rope(y):
        half = ROPE_AXIS_DIM // 2
        partner = jnp.where(first_half, pltpu.roll(y, LANES - half, 1), pltpu.roll(y, half, 1))
        return y * cos + partner * sin

    o0 = ATTN_WIDTH
    o1 = o0 + KV_WIDTH
    o2 = o1 + KV_WIDTH
    o3 = o2 + FOURIER_WIDTH
    d_model = x.shape[-1]
    tile = 2 * LANES

    zq = _dot(h, w_ref[:, :o0])
    for c in range(ATTN_WIDTH // tile):
        yn = head_norm(zq[:, c * tile:(c + 1) * tile])
        for cc in range(tile // LANES):
            yq = rope(yn[:, cc * LANES:(cc + 1) * LANES] * gq_ref[...]) * Q_SCALE
            r0 = c * tile + cc * LANES
            qt_ref[0, r0:r0 + LANES, :] = yq.T.astype(BF16)

    zkv = _dot(h, w_ref[:, o0:o2])
    yk = rope(head_norm(zkv)[:, :KV_WIDTH] * gk_ref[...])
    low = lane < HEAD_DIM
    kbias = kbias_ref[...]
    k_ref[0, 0] = jnp.where(low, yk, kbias).astype(BF16)
    k_ref[0, 1] = jnp.where(low, pltpu.roll(yk, HEAD_DIM, 1), kbias).astype(BF16)

    vt_ref[0] = zkv[:, KV_WIDTH:].T.astype(BF16)

    zf32 = _dot(h, w_ref[:, o2:o3])
    zf = zf32.astype(BF16)
    if z_ref.shape[-1] == FOURIER_WIDTH:
        z_ref[0] = zf
    elif z_ref.shape[-1] == FFT_RADIX * FOURIER_WIDTH:
        slabs, = slab_ref
        rows = z_ref.shape[1]
        for g in range(N_FOURIER_GROUPS):
            slabs[g] = zf32[:, g * FOURIER_GROUP:(g + 1) * FOURIER_GROUP]
        for j in range(FFT_RADIX):
            for g in range(N_FOURIER_GROUPS):
                lo = j * FOURIER_WIDTH + g * FOURIER_GROUP
                z_ref[0, :, lo:lo + FOURIER_GROUP] = slabs[g, pl.ds(j, rows, stride=FFT_RADIX), :].astype(BF16)
    else:
        for g in range(N_FOURIER_GROUPS):
            f = zf[:, g * FOURIER_GROUP:(g + 1) * FOURIER_GROUP]
            zc = _dot(f, wc_ref[...])
            z_ref[0, :, g * FOURIER_GROUP:(g + 1) * FOURIER_GROUP] = zc[:, :FOURIER_GROUP].astype(BF16)
            z_ref[0, :, FOURIER_WIDTH + g * FOURIER_GROUP:FOURIER_WIDTH + (g + 1) * FOURIER_GROUP] = (
                zc[:, FOURIER_GROUP:].astype(BF16))

    gate_chunk = 512
    for c in range(2 * d_model // gate_chunk):
        zg = _dot(h, w_ref[:, o3 + c * gate_chunk:o3 + (c + 1) * gate_chunk])
        gate_ref[0, :, c * gate_chunk:(c + 1) * gate_chunk] = (1.0 / (1.0 + jnp.exp(-zg))).astype(BF16)


def _in_proj(x, norm_g, w_in, cos, sin, gq, gk, kbias, hmean, wc, tm, fourier):
    b, n, d = x.shape
    in_width = w_in.shape[1]
    grid = (b, n // tm)
    scratch = []
    if fourier == "residue_major":
        four_rows, four_tile, four_width = n // FFT_RADIX, tm // FFT_RADIX, FFT_RADIX * FOURIER_WIDTH
        scratch = [pltpu.VMEM((N_FOURIER_GROUPS, tm, FOURIER_GROUP), F32)]
    else:
        four_rows, four_tile = n, tm
        four_width = 2 * FOURIER_WIDTH if fourier == "channel_dft" else FOURIER_WIDTH
    out_shape = (
        jax.ShapeDtypeStruct((b, ATTN_WIDTH, n), BF16),
        jax.ShapeDtypeStruct((b, N_KV_HEADS, n, LANES), BF16),
        jax.ShapeDtypeStruct((b, KV_WIDTH, n), BF16),
        jax.ShapeDtypeStruct((b, four_rows, four_width), BF16),
        jax.ShapeDtypeStruct((b, n, 2 * d), BF16),
    )
    return pl.pallas_call(
        _in_proj_kernel,
        grid=grid,
        in_specs=[
            pl.BlockSpec((1, tm, d), lambda i, j: (i, j, 0)),
            _const_spec((1, d)),
            _const_spec((d, in_width)),
            pl.BlockSpec((tm, LANES), lambda i, j: (j, 0)),
            pl.BlockSpec((tm, LANES), lambda i, j: (j, 0)),
            _const_spec((1, LANES)),
            _const_spec((1, LANES)),
            _const_spec((1, LANES)),
            _const_spec((2 * LANES, 2 * LANES)),
            _const_spec((FOURIER_GROUP, 2 * FOURIER_GROUP)),
        ],
        out_specs=(
            pl.BlockSpec((1, ATTN_WIDTH, tm), lambda i, j: (i, 0, j)),
            pl.BlockSpec((1, N_KV_HEADS, tm, LANES), lambda i, j: (i, 0, j, 0)),
            pl.BlockSpec((1, KV_WIDTH, tm), lambda i, j: (i, 0, j)),
            pl.BlockSpec((1, four_tile, four_width), lambda i, j: (i, j, 0)),
            pl.BlockSpec((1, tm, 2 * d), lambda i, j: (i, j, 0)),
        ),
        out_shape=out_shape,
        scratch_shapes=scratch,
        compiler_params=_params(2),
        name="in_proj",
    )(x, norm_g, w_in, cos, sin, gq, gk, kbias, hmean, wc)


def _attn_kernel(qt_ref, k_ref, vt_ref, km_ref, vtm_ref, ot_ref, *, tk, bounded):
    tq = qt_ref.shape[2]
    n_keys = k_ref.shape[2]
    n_blocks = n_keys // tk
    heads = range(Q_PER_KV)

    def head_qt(h, lo, width):
        ones_row = (lax.broadcasted_iota(jnp.int32, (HEAD_DIM, width), 0) == 0).astype(BF16)
        return jnp.concatenate([qt_ref[0, h * HEAD_DIM:(h + 1) * HEAD_DIM, lo:lo + width], ones_row], axis=0)

    if bounded:
        sub = min(tq, BOUNDED_Q_SUBTILE)
        for lo in range(0, tq, sub):
            qts = [head_qt(h, lo, sub) for h in heads]
            p = [jnp.exp2(_dot(km_ref[0, 0], qts[h])) for h in heads]
            l = [jnp.sum(p[h], axis=0, keepdims=True) for h in heads]
            acc = [_dot(vtm_ref[0], p[h].astype(BF16)) for h in heads]
            s = [_dot(k_ref[0, 0, 0:tk, :], qts[h]) for h in heads]
            for j in range(n_blocks):
                for h in heads:
                    ph = jnp.exp2(s[h])
                    l[h] = l[h] + jnp.sum(ph, axis=0, keepdims=True)
                    if j + 1 < n_blocks:
                        s[h] = _dot(k_ref[0, 0, (j + 1) * tk:(j + 2) * tk, :], qts[h])
                    acc[h] = acc[h] + _dot(vt_ref[0, :, j * tk:(j + 1) * tk], ph.astype(BF16))
            for h in heads:
                ot_ref[0, h * HEAD_DIM:(h + 1) * HEAD_DIM, lo:lo + sub] = (acc[h] / l[h]).astype(BF16)
        return

    qt = jnp.concatenate([head_qt(h, 0, tq) for h in heads], axis=1)

    def block(kb, vb, m, l, acc):
        s = _dot(kb, qt)
        m_new = jnp.maximum(m, jnp.max(s, axis=0, keepdims=True))
        alpha = jnp.exp2(m - m_new)
        p = jnp.exp2(s - m_new)
        return (m_new, alpha * l + jnp.sum(p, axis=0, keepdims=True), alpha * acc + _dot(vb, p.astype(BF16)))

    width = qt.shape[1]
    init = (jnp.full((1, width), -jnp.inf, F32), jnp.zeros((1, width), F32), jnp.zeros((HEAD_DIM, width), F32))
    carry = block(km_ref[0, 0], vtm_ref[0], *init)

    def body(j, carry):
        start = pl.multiple_of(j * tk, tk)
        return block(k_ref[0, 0, pl.ds(start, tk), :], vt_ref[0, :, pl.ds(start, tk)], *carry)

    _, l, acc = lax.fori_loop(0, n_blocks, body, carry)
    out = (acc / l).astype(BF16)
    for h in range(Q_PER_KV):
        ot_ref[0, h * HEAD_DIM:(h + 1) * HEAD_DIM, :] = out[:, h * tq:(h + 1) * tq]


def _attention(qt, k, vt, k_meta, vt_meta, tq, tk, bounded):
    b, _, n = qt.shape
    group_rows = Q_PER_KV * HEAD_DIM
    return pl.pallas_call(
        functools.partial(_attn_kernel, tk=tk, bounded=bounded),
        grid=(b, N_KV_HEADS, n // tq),
        in_specs=[
            pl.BlockSpec((1, group_rows, tq), lambda i, g, j: (i, g, j)),
            pl.BlockSpec((1, 1, n, LANES), lambda i, g, j: (i, g, 0, 0)),
            pl.BlockSpec((1, HEAD_DIM, n), lambda i, g, j: (i, g, 0)),
            pl.BlockSpec((1, 1, N_META, LANES), lambda i, g, j: (0, g, 0, 0)),
            pl.BlockSpec((1, HEAD_DIM, N_META), lambda i, g, j: (0, g, 0)),
        ],
        out_specs=pl.BlockSpec((1, group_rows, tq), lambda i, g, j: (i, g, j)),
        out_shape=jax.ShapeDtypeStruct((b, ATTN_WIDTH, n), BF16),
        compiler_params=_params(3),
        name="attention",
    )(qt, k, vt, k_meta, vt_meta)


def _seq_dft_kernel(mc_ref, ms_ref, z_ref, cm_ref, sm_ref, zm_ref, out_ref, acc_ref):
    kk = pl.program_id(2)

    @pl.when(kk == 0)
    def _():
        zm = zm_ref[0]
        acc_ref[...] = (_dot(cm_ref[...], zm[:, :FOURIER_WIDTH]) + _dot(sm_ref[...], zm[:, FOURIER_WIDTH:]))

    z = z_ref[0]
    acc_ref[...] += _dot(mc_ref[...], z[:, :FOURIER_WIDTH]) + _dot(ms_ref[...], z[:, FOURIER_WIDTH:])

    @pl.when(kk == pl.num_programs(2) - 1)
    def _():
        out_ref[0] = acc_ref[...].astype(out_ref.dtype)


def _seq_dft(z, z_meta, mc, ms, cm, sm, tm, tk):
    b, n, _ = z.shape
    return pl.pallas_call(
        _seq_dft_kernel,
        grid=(n // tm, b, n // tk),
        in_specs=[
            pl.BlockSpec((tm, tk), lambda r, i, kk: (r, kk)),
            pl.BlockSpec((tm, tk), lambda r, i, kk: (r, kk)),
            pl.BlockSpec((1, tk, 2 * FOURIER_WIDTH), lambda r, i, kk: (i, kk, 0)),
            pl.BlockSpec((tm, N_META), lambda r, i, kk: (r, 0)),
            pl.BlockSpec((tm, N_META), lambda r, i, kk: (r, 0)),
            pl.BlockSpec((1, N_META, 2 * FOURIER_WIDTH), lambda r, i, kk: (0, 0, 0)),
        ],
        out_specs=pl.BlockSpec((1, tm, FOURIER_WIDTH), lambda r, i, kk: (i, r, 0)),
        out_shape=jax.ShapeDtypeStruct((b, n, FOURIER_WIDTH), BF16),
        scratch_shapes=[pltpu.VMEM((tm, FOURIER_WIDTH), F32)],
        compiler_params=_params(3),
        name="seq_dft",
    )(mc, ms, z, cm, sm, z_meta)


def _fft_a_kernel(e_ref, x_ref, t_ref, xm_ref, out_ref, meta_ref):
    n1p = out_ref.shape[1]
    width = FOURIER_WIDTH

    @pl.when(pl.program_id(1) == 0)
    def _():
        for q in range(FFT_A_RESIDUES_PER_STEP):
            meta_ref[q] = _dot(t_ref[q], xm_ref[...])

    for q in range(FFT_A_RESIDUES_PER_STEP):
        r = _dot(e_ref[q], x_ref[0, :, q * width:(q + 1) * width]) + meta_ref[q]
        for g in range(N_FOURIER_GROUPS):
            base = (q * N_FOURIER_GROUPS + g) * 2 * FOURIER_GROUP
            cols = slice(g * FOURIER_GROUP, (g + 1) * FOURIER_GROUP)
            out_ref[0, :, base:base + FOURIER_GROUP] = r[:n1p, cols].astype(BF16)
            out_ref[0, :, base + FOURIER_GROUP:base + 2 * FOURIER_GROUP] = r[n1p:, cols].astype(BF16)


def _fft_b_kernel(ab_ref, mk_ref, out_ref):
    per_residue = N_FOURIER_GROUPS * 2 * FOURIER_GROUP
    for g in range(N_FOURIER_GROUPS):
        lo = g * 2 * FOURIER_GROUP
        lhs = jnp.concatenate([ab_ref[0, :, q * per_residue + lo:q * per_residue + lo + 2 * FOURIER_GROUP]
                               for q in range(FFT_RADIX)], axis=1)
        res = _dot(lhs, mk_ref[...])
        for k2 in range(FFT_RADIX):
            out_ref[0, k2, :, g * FOURIER_GROUP:(g + 1) * FOURIER_GROUP] = (
                res[:, k2 * FOURIER_GROUP:(k2 + 1) * FOURIER_GROUP].astype(BF16))


def _fft_row_tile(n1p):
    return max(t for t in range(16, 193, 16) if n1p % t == 0)


def _seq_fft(xv, f_meta, e, t, mk):
    b, m, _ = xv.shape
    width = FOURIER_WIDTH
    n1 = m + 1
    n1p = e.shape[1] // 2
    qs = FFT_A_RESIDUES_PER_STEP
    per_residue = N_FOURIER_GROUPS * 2 * FOURIER_GROUP
    ab = pl.pallas_call(
        _fft_a_kernel,
        grid=(FFT_RADIX // qs, b),
        in_specs=[
            pl.BlockSpec((qs, 2 * n1p, m), lambda q, i: (q, 0, 0)),
            pl.BlockSpec((1, m, qs * width), lambda q, i: (i, 0, q)),
            pl.BlockSpec((qs, 2 * n1p, FFT_RADIX), lambda q, i: (q, 0, 0)),
            pl.BlockSpec((FFT_RADIX, width), lambda q, i: (0, 0)),
        ],
        out_specs=pl.BlockSpec((1, n1p, qs * per_residue), lambda q, i: (i, 0, q)),
        out_shape=jax.ShapeDtypeStruct((b, n1p, FFT_RADIX * per_residue), BF16),
        scratch_shapes=[pltpu.VMEM((qs, 2 * n1p, width), F32)],
        compiler_params=_params(2),
        name="seq_fft_a",
    )(e, xv, t, f_meta)
    tr = _fft_row_tile(n1p)
    out = pl.pallas_call(
        _fft_b_kernel,
        grid=(b, n1p // tr),
        in_specs=[
            pl.BlockSpec((1, tr, FFT_RADIX * per_residue), lambda i, r: (i, r, 0)),
            _const_spec(mk.shape),
        ],
        out_specs=pl.BlockSpec((1, FFT_RADIX, tr, width), lambda i, r: (i, 0, r, 0)),
        out_shape=jax.ShapeDtypeStruct((b, FFT_RADIX, n1p, width), BF16),
        compiler_params=_params(2),
        name="seq_fft_b",
    )(ab, mk)
    return jnp.concatenate([out[:, k2, (N_META if k2 == 0 else 0):n1] for k2 in range(FFT_RADIX)], axis=1)


def _seq_fft_tables(n):
    length = n + N_META
    m = n // FFT_RADIX
    n1 = m + 1
    n1p = -(-n1 // 16) * 16
    k1 = jnp.arange(n1p, dtype=jnp.int32)
    n2 = jnp.arange(FFT_RADIX, dtype=jnp.int32)
    valid = (k1 < n1).astype(F32)
    c1, s1 = _unit_circle(k1[:, None] * (FFT_RADIX * (jnp.arange(m, dtype=jnp.int32) + 1))[None, :], length)
    c2, s2 = _unit_circle(n2[:, None] * k1[None, :], length)
    c2, s2 = c2 * valid[None, :], s2 * valid[None, :]
    cos = c1[None, :, :] * c2[:, :, None] - s1[None, :, :] * s2[:, :, None]
    sin = s1[None, :, :] * c2[:, :, None] + c1[None, :, :] * s2[:, :, None]
    e = jnp.concatenate([cos, -sin], axis=1).astype(BF16)
    pick = (n2[:, None, None] == n2[None, None, :]).astype(F32)
    t = jnp.concatenate([c2[:, :, None] * pick, -s2[:, :, None] * pick], axis=1).astype(BF16)
    ch = jnp.arange(FOURIER_GROUP, dtype=jnp.int32)
    ca, sa = _unit_circle(n2[:, None] * n2[None, :], FFT_RADIX)
    cb, sb = _unit_circle(ch[:, None] * ch[None, :], FOURIER_GROUP)
    ck = ca[:, None, :, None] * cb[None, :, None, :] - sa[:, None, :, None] * sb[None, :, None, :]
    sk = sa[:, None, :, None] * cb[None, :, None, :] + ca[:, None, :, None] * sb[None, :, None, :]
    scale = 1.0 / math.sqrt(length * FOURIER_GROUP)
    mk = (jnp.stack([ck, sk], axis=1) * scale).reshape(FFT_RADIX * 2 * FOURIER_GROUP, FFT_RADIX * FOURIER_GROUP)
    return e, t, mk.astype(BF16)


def _rms(x, g):
    return x * lax.rsqrt(jnp.mean(x * x, axis=-1, keepdims=True) + EPS) * g


def _mix_ffn_kernel(x_ref, ot_ref, four_ref, gate_ref, wao_ref, wfo_ref, wout_ref, g2_ref, wfi_ref, wfo2_ref,
                    gfin_ref, y_ref, *, ff_chunk):
    x = x_ref[0]
    d_model = x.shape[-1]
    d_ff = wfo2_ref.shape[0]
    attn = ot_ref[0].astype(F32).T.astype(BF16)
    a = _dot(attn, wao_ref[...])
    fo = _dot(four_ref[0], wfo_ref[...])
    gates = gate_ref[0]
    merged = gates[:, :d_model].astype(F32) * a + gates[:, d_model:].astype(F32) * fo
    x1 = x + _dot(merged.astype(BF16), wout_ref[...])
    h2 = _rms(x1, g2_ref[...]).astype(BF16)
    acc = x1
    for c in range(d_ff // ff_chunk):
        gt = _dot(h2, wfi_ref[:, c * ff_chunk:(c + 1) * ff_chunk])
        up = _dot(h2, wfi_ref[:, d_ff + c * ff_chunk:d_ff + (c + 1) * ff_chunk])
        act = gt * (1.0 / (1.0 + jnp.exp(-gt))) * up
        acc = acc + _dot(act.astype(BF16), wfo2_ref[c * ff_chunk:(c + 1) * ff_chunk, :])
    y_ref[0] = _rms(acc, gfin_ref[...])


def _mix_ffn(x, ot, four, gates, w_attn_o, w_four_o, w_out, g2, w_ffn_in, w_ffn_out, gfin, tm, ff_chunk):
    b, n, d = x.shape
    d_ff = w_ffn_out.shape[0]
    return pl.pallas_call(
        functools.partial(_mix_ffn_kernel, ff_chunk=ff_chunk),
        grid=(b, n // tm),
        in_specs=[
            pl.BlockSpec((1, tm, d), lambda i, j: (i, j, 0)),
            pl.BlockSpec((1, ATTN_WIDTH, tm), lambda i, j: (i, 0, j)),
            pl.BlockSpec((1, tm, FOURIER_WIDTH), lambda i, j: (i, j, 0)),
            pl.BlockSpec((1, tm, 2 * d), lambda i, j: (i, j, 0)),
            _const_spec((ATTN_WIDTH, d)),
            _const_spec((FOURIER_WIDTH, d)),
            _const_spec((d, d)),
            _const_spec((1, d)),
            _const_spec((d, 2 * d_ff)),
            _const_spec((d_ff, d)),
            _const_spec((1, d)),
        ],
        out_specs=pl.BlockSpec((1, tm, d), lambda i, j: (i, j, 0)),
        out_shape=jax.ShapeDtypeStruct((b, n, d), F32),
        compiler_params=_params(2),
        name="mix_ffn",
    )(x, ot, four, gates, w_attn_o, w_four_o, w_out, g2, w_ffn_in, w_ffn_out, gfin)


def _rope_tables(row, col):
    inv_freq = 1.0 / (ROPE_THETA ** (jnp.arange(0, ROPE_AXIS_DIM, 2, dtype=F32) / ROPE_AXIS_DIM))
    ar = row[:, None] * inv_freq[None, :]
    ac = col[:, None] * inv_freq[None, :]
    cos = jnp.concatenate([jnp.cos(ar), jnp.cos(ar), jnp.cos(ac), jnp.cos(ac)], axis=-1)
    sin = jnp.concatenate([-jnp.sin(ar), jnp.sin(ar), -jnp.sin(ac), jnp.sin(ac)], axis=-1)
    reps = LANES // HEAD_DIM
    return jnp.tile(cos, (1, reps)), jnp.tile(sin, (1, reps))


def _unit_circle(prod, length):
    ang = (prod % length).astype(F32) * (2.0 * math.pi / length)
    return jnp.cos(ang), jnp.sin(ang)


def _seq_dft_table(n, pos):
    length = n + N_META
    coarse = (jnp.arange(n // GRID_W, dtype=jnp.int32) * GRID_W)[:, None] * pos[None, :]
    fine = (jnp.arange(GRID_W, dtype=jnp.int32) + N_META)[:, None] * pos[None, :]
    c1, s1 = _unit_circle(coarse, length)
    c2, s2 = _unit_circle(fine, length)
    cos = (c1[:, None, :] * c2[None, :, :] - s1[:, None, :] * s2[None, :, :]).reshape(n, pos.shape[0])
    sin = (s1[:, None, :] * c2[None, :, :] + c1[:, None, :] * s2[None, :, :]).reshape(n, pos.shape[0])
    return cos.astype(BF16), sin.astype(BF16)


def _seq_dft_tables(n):
    mc, ms = _seq_dft_table(n, jnp.arange(n, dtype=jnp.int32) + N_META)
    cm, sm = _seq_dft_table(n, jnp.arange(N_META, dtype=jnp.int32))
    return mc, ms, cm, sm


def _channel_dft_table(length):
    idx = jnp.arange(FOURIER_GROUP, dtype=jnp.int32)
    c, s = _unit_circle(idx[:, None] * idx[None, :], FOURIER_GROUP)
    scale = 1.0 / math.sqrt(length * FOURIER_GROUP)
    return (jnp.concatenate([c, -s], axis=1) * scale).astype(BF16)


def _encode(x, meta_tokens, p):
    b, n, d = x.shape
    length = n + N_META
    tok = jnp.arange(n, dtype=jnp.int32)
    cos, sin = _rope_tables((tok // GRID_W).astype(F32), (tok % GRID_W).astype(F32))
    meta_pad = LANES
    mrow = jnp.full((meta_pad,), -1.0, F32)
    mcol = jnp.arange(meta_pad, dtype=F32)
    cos_m, sin_m = _rope_tables(mrow, mcol)
    wc = _channel_dft_table(length)
    x_meta = jnp.zeros((1, meta_pad, d), F32).at[0, :N_META].set(meta_tokens)

    dense_dft = n < FFT_MIN_TOKENS
    proj_consts = (p["norm_mix_g"], p["w_in"])
    head_consts = (p["gq"], p["gk"], p["kbias"], p["hmean"], wc)
    _, k_m, vt_m, z_m, _ = _in_proj(x_meta, *proj_consts, cos_m, sin_m, *head_consts, tm=meta_pad,
                                    fourier="channel_dft" if dense_dft else "raw")
    qt, k, vt, z, gates = _in_proj(x, *proj_consts, cos, sin, *head_consts, tm=min(n, 512),
                                   fourier="channel_dft" if dense_dft else "residue_major")

    attn = functools.partial(_attention, qt, k, vt, k_m[:, :, :N_META], vt_m[:, :, :N_META])
    ot = lax.cond(p["bounded"],
                  functools.partial(attn, tq=min(n, 2 * BOUNDED_Q_SUBTILE), tk=min(n, 256), bounded=True),
                  functools.partial(attn, tq=min(n, 256), tk=min(n, 512), bounded=False))

    if dense_dft:
        mc, ms, cm, sm = _seq_dft_tables(n)
        four = _seq_dft(z, z_m[:, :N_META], mc, ms, cm, sm, tm=min(n, 2048), tk=min(n, 1024))
    else:
        four = _seq_fft(z, z_m[0, :N_META], *_seq_fft_tables(n))

    return _mix_ffn(x, ot, four, gates, p["w_attn_o"], p["w_four_o"], p["w_out"], p["norm_ffn_g"],
                    p["w_ffn_in"], p["w_ffn_out"], p["final_norm_g"], tm=min(n, 256),
                    ff_chunk=p["w_ffn_out"].shape[0])


def kernel(x_prompt, x_sample, meta_tokens, norm_mix_g, w_in, q_norm_g, k_norm_g, w_attn_o, w_four_o, w_out,
           norm_ffn_g, w_ffn_in, w_ffn_out, final_norm_g):
    assert w_in.shape[0] == 1, "single-layer encoder: meta rows are only needed as keys / DFT inputs"
    reps = LANES // HEAD_DIM
    head_id = jnp.arange(2 * LANES) // HEAD_DIM
    bound = 1.01 * LOG2E * HEAD_DIM ** 0.5 * jnp.max(jnp.abs(q_norm_g[0])) * jnp.max(jnp.abs(k_norm_g[0]))
    bounded = bound <= MAX_SCORE_BOUND_LOG2
    p = {
        "bounded": bounded,
        "kbias": jnp.where((jnp.arange(LANES) == HEAD_DIM) & bounded, -bound, 0.0).astype(F32)[None, :],
        "norm_mix_g": norm_mix_g[0][None, :],
        "w_in": w_in[0].astype(BF16),
        "gq": jnp.tile(q_norm_g[0], reps)[None, :],
        "gk": jnp.tile(k_norm_g[0], reps)[None, :],
        "hmean": ((head_id[:, None] == head_id[None, :]).astype(F32) / HEAD_DIM).astype(BF16),
        "w_attn_o": w_attn_o[0].astype(BF16),
        "w_four_o": w_four_o[0].astype(BF16),
        "w_out": w_out[0].astype(BF16),
        "norm_ffn_g": norm_ffn_g[0][None, :],
        "w_ffn_in": w_ffn_in[0].astype(BF16),
        "w_ffn_out": w_ffn_out[0].astype(BF16),
        "final_norm_g": final_norm_g[None, :],
    }
    return (_encode(x_prompt, meta_tokens, p), _encode(x_sample, meta_tokens, p))
```

```python
import functools
import math

import jax
import jax.numpy as jnp
from jax import lax
from jax.experimental import pallas as pl
from jax.experimental.pallas import tpu as pltpu

N_META = 16
GRID_W = 64
HEAD_DIM = 64
N_Q_HEADS = 8
N_KV_HEADS = 2
Q_PER_KV = N_Q_HEADS // N_KV_HEADS
ATTN_WIDTH = N_Q_HEADS * HEAD_DIM
KV_WIDTH = N_KV_HEADS * HEAD_DIM
FOURIER_WIDTH = 512
FOURIER_GROUP = 128
N_FOURIER_GROUPS = FOURIER_WIDTH // FOURIER_GROUP
ROPE_AXIS_DIM = HEAD_DIM // 2
ROPE_THETA = 10000.0
EPS = 1e-6

LANES = 128
VMEM_LIMIT_BYTES = 56 * 1024 * 1024

F32 = jnp.float32
BF16 = jnp.bfloat16

LOG2E = math.log2(math.e)
Q_SCALE = HEAD_DIM ** -0.5 * LOG2E
MAX_SCORE_BOUND_LOG2 = 60.0
BOUNDED_Q_SUBTILE = 2 * LANES

FFT_RADIX = N_META
FFT_A_RESIDUES_PER_STEP = 4
FFT_MIN_TOKENS = 4096


def _dot(a, b):
    return jnp.dot(a, b, preferred_element_type=F32)


def _const_spec(shape):
    zeros = (0,) * len(shape)
    return pl.BlockSpec(shape, lambda *_: zeros, pipeline_mode=pl.Buffered(1))


def _params(n_axes):
    return pltpu.CompilerParams(dimension_semantics=("arbitrary",) * n_axes,
                                vmem_limit_bytes=VMEM_LIMIT_BYTES)


def _in_proj_kernel(x_ref, g_ref, w_ref, cos_ref, sin_ref, gq_ref, gk_ref, kbias_ref, hmean_ref, wc_ref,
                    qt_ref, k_ref, vt_ref, z_ref, gate_ref, *slab_ref):
    x = x_ref[0]
    h = (x * lax.rsqrt(jnp.mean(x * x, axis=-1, keepdims=True) + EPS) * g_ref[...]).astype(BF16)
    cos = cos_ref[...]
    sin = sin_ref[...]
    lane = lax.broadcasted_iota(jnp.int32, cos.shape, 1)
    first_half = (lane % ROPE_AXIS_DIM) < (ROPE_AXIS_DIM // 2)

    def head_norm(z2):
        return z2 * lax.rsqrt(_dot((z2 * z2).astype(BF16), hmean_ref[...]) + EPS)

    def rope(y):
        half = ROPE_AXIS_DIM // 2
        partner = jnp.where(first_half, pltpu.roll(y, LANES - half, 1), pltpu.roll(y, half, 1))
        return y * cos + partner * sin

    o0 = ATTN_WIDTH
    o1 = o0 + KV_WIDTH
    o2 = o1 + KV_WIDTH
    o3 = o2 + FOURIER_WIDTH
    d_model = x.shape[-1]
    tile = 2 * LANES

    zq = _dot(h, w_ref[:, :o0])
    for c in range(ATTN_WIDTH // tile):
        yn = head_norm(zq[:, c * tile:(c + 1) * tile])
        for cc in range(tile // LANES):
            yq = rope(yn[:, cc * LANES:(cc + 1) * LANES] * gq_ref[...]) * Q_SCALE
            r0 = c * tile + cc * LANES
            qt_ref[0, r0:r0 + LANES, :] = yq.T.astype(BF16)

    zkv = _dot(h, w_ref[:, o0:o2])
    yk = rope(head_norm(zkv)[:, :KV_WIDTH] * gk_ref[...])
    low = lane < HEAD_DIM
    kbias = kbias_ref[...]
    k_ref[0, 0] = jnp.where(low, yk, kbias).astype(BF16)
    k_ref[0, 1] = jnp.where(low, pltpu.roll(yk, HEAD_DIM, 1), kbias).astype(BF16)

    vt_ref[0] = zkv[:, KV_WIDTH:].T.astype(BF16)

    zf32 = _dot(h, w_ref[:, o2:o3])
    zf = zf32.astype(BF16)
    if z_ref.shape[-1] == FOURIER_WIDTH:
        z_ref[0] = zf
    elif z_ref.shape[-1] == FFT_RADIX * FOURIER_WIDTH:
        slabs, = slab_ref
        rows = z_ref.shape[1]
        for g in range(N_FOURIER_GROUPS):
            slabs[g] = zf32[:, g * FOURIER_GROUP:(g + 1) * FOURIER_GROUP]
        for j in range(FFT_RADIX):
            for g in range(N_FOURIER_GROUPS):
                lo = j * FOURIER_WIDTH + g * FOURIER_GROUP
                z_ref[0, :, lo:lo + FOURIER_GROUP] = slabs[g, pl.ds(j, rows, stride=FFT_RADIX), :].astype(BF16)
    else:
        for g in range(N_FOURIER_GROUPS):
            f = zf[:, g * FOURIER_GROUP:(g + 1) * FOURIER_GROUP]
            zc = _dot(f, wc_ref[...])
            z_ref[0, :, g * FOURIER_GROUP:(g + 1) * FOURIER_GROUP] = zc[:, :FOURIER_GROUP].astype(BF16)
            z_ref[0, :, FOURIER_WIDTH + g * FOURIER_GROUP:FOURIER_WIDTH + (g + 1) * FOURIER_GROUP] = (
                zc[:, FOURIER_GROUP:].astype(BF16))

    gate_chunk = 512
    for c in range(2 * d_model // gate_chunk):
        zg = _dot(h, w_ref[:, o3 + c * gate_chunk:o3 + (c + 1) * gate_chunk])
        gate_ref[0, :, c * gate_chunk:(c + 1) * gate_chunk] = (1.0 / (1.0 + jnp.exp(-zg))).astype(BF16)


def _in_proj(x, norm_g, w_in, cos, sin, gq, gk, kbias, hmean, wc, tm, fourier):
    b, n, d = x.shape
    in_width = w_in.shape[1]
    grid = (b, n // tm)
    scratch = []
    if fourier == "residue_major":
        four_rows, four_tile, four_width = n // FFT_RADIX, tm // FFT_RADIX, FFT_RADIX * FOURIER_WIDTH
        scratch = [pltpu.VMEM((N_FOURIER_GROUPS, tm, FOURIER_GROUP), F32)]
    else:
        four_rows, four_tile = n, tm
        four_width = 2 * FOURIER_WIDTH if fourier == "channel_dft" else FOURIER_WIDTH
    out_shape = (
        jax.ShapeDtypeStruct((b, ATTN_WIDTH, n), BF16),
        jax.ShapeDtypeStruct((b, N_KV_HEADS, n, LANES), BF16),
        jax.ShapeDtypeStruct((b, KV_WIDTH, n), BF16),
        jax.ShapeDtypeStruct((b, four_rows, four_width), BF16),
        jax.ShapeDtypeStruct((b, n, 2 * d), BF16),
    )
    return pl.pallas_call(
        _in_proj_kernel,
        grid=grid,
        in_specs=[
            pl.BlockSpec((1, tm, d), lambda i, j: (i, j, 0)),
            _const_spec((1, d)),
            _const_spec((d, in_width)),
            pl.BlockSpec((tm, LANES), lambda i, j: (j, 0)),
            pl.BlockSpec((tm, LANES), lambda i, j: (j, 0)),
            _const_spec((1, LANES)),
            _const_spec((1, LANES)),
            _const_spec((1, LANES)),
            _const_spec((2 * LANES, 2 * LANES)),
            _const_spec((FOURIER_GROUP, 2 * FOURIER_GROUP)),
        ],
        out_specs=(
            pl.BlockSpec((1, ATTN_WIDTH, tm), lambda i, j: (i, 0, j)),
            pl.BlockSpec((1, N_KV_HEADS, tm, LANES), lambda i, j: (i, 0, j, 0)),
            pl.BlockSpec((1, KV_WIDTH, tm), lambda i, j: (i, 0, j)),
            pl.BlockSpec((1, four_tile, four_width), lambda i, j: (i, j, 0)),
            pl.BlockSpec((1, tm, 2 * d), lambda i, j: (i, j, 0)),
        ),
        out_shape=out_shape,
        scratch_shapes=scratch,
        compiler_params=_params(2),
        name="in_proj",
    )(x, norm_g, w_in, cos, sin, gq, gk, kbias, hmean, wc)


def _attn_kernel(qt_ref, k_ref, vt_ref, km_ref, vtm_ref, ot_ref, *, tk, bounded):
    tq = qt_ref.shape[2]
    n_keys = k_ref.shape[2]
    n_blocks = n_keys // tk
    heads = range(Q_PER_KV)

    def head_qt(h, lo, width):
        ones_row = (lax.broadcasted_iota(jnp.int32, (HEAD_DIM, width), 0) == 0).astype(BF16)
        return jnp.concatenate([qt_ref[0, h * HEAD_DIM:(h + 1) * HEAD_DIM, lo:lo + width], ones_row], axis=0)

    if bounded:
        sub = min(tq, BOUNDED_Q_SUBTILE)
        for lo in range(0, tq, sub):
            qts = [head_qt(h, lo, sub) for h in heads]
            p = [jnp.exp2(_dot(km_ref[0, 0], qts[h])) for h in heads]
            l = [jnp.sum(p[h], axis=0, keepdims=True) for h in heads]
            acc = [_dot(vtm_ref[0], p[h].astype(BF16)) for h in heads]
            s = [_dot(k_ref[0, 0, 0:tk, :], qts[h]) for h in heads]
            for j in range(n_blocks):
                for h in heads:
                    ph = jnp.exp2(s[h])
                    l[h] = l[h] + jnp.sum(ph, axis=0, keepdims=True)
                    if j + 1 < n_blocks:
                        s[h] = _dot(k_ref[0, 0, (j + 1) * tk:(j + 2) * tk, :], qts[h])
                    acc[h] = acc[h] + _dot(vt_ref[0, :, j * tk:(j + 1) * tk], ph.astype(BF16))
            for h in heads:
                ot_ref[0, h * HEAD_DIM:(h + 1) * HEAD_DIM, lo:lo + sub] = (acc[h] / l[h]).astype(BF16)
        return

    qt = jnp.concatenate([head_qt(h, 0, tq) for h in heads], axis=1)

    def block(kb, vb, m, l, acc):
        s = _dot(kb, qt)
        m_new = jnp.maximum(m, jnp.max(s, axis=0, keepdims=True))
        alpha = jnp.exp2(m - m_new)
        p = jnp.exp2(s - m_new)
        return (m_new, alpha * l + jnp.sum(p, axis=0, keepdims=True), alpha * acc + _dot(vb, p.astype(BF16)))

    width = qt.shape[1]
    init = (jnp.full((1, width), -jnp.inf, F32), jnp.zeros((1, width), F32), jnp.zeros((HEAD_DIM, width), F32))
    carry = block(km_ref[0, 0], vtm_ref[0], *init)

    def body(j, carry):
        start = pl.multiple_of(j * tk, tk)
        return block(k_ref[0, 0, pl.ds(start, tk), :], vt_ref[0, :, pl.ds(start, tk)], *carry)

    _, l, acc = lax.fori_loop(0, n_blocks, body, carry)
    out = (acc / l).astype(BF16)
    for h in range(Q_PER_KV):
        ot_ref[0, h * HEAD_DIM:(h + 1) * HEAD_DIM, :] = out[:, h * tq:(h + 1) * tq]


def _attention(qt, k, vt, k_meta, vt_meta, tq, tk, bounded):
    b, _, n = qt.shape
    group_rows = Q_PER_KV * HEAD_DIM
    return pl.pallas_call(
        functools.partial(_attn_kernel, tk=tk, bounded=bounded),
        grid=(b, N_KV_HEADS, n // tq),
        in_specs=[
            pl.BlockSpec((1, group_rows, tq), lambda i, g, j: (i, g, j)),
            pl.BlockSpec((1, 1, n, LANES), lambda i, g, j: (i, g, 0, 0)),
            pl.BlockSpec((1, HEAD_DIM, n), lambda i, g, j: (i, g, 0)),
            pl.BlockSpec((1, 1, N_META, LANES), lambda i, g, j: (0, g, 0, 0)),
            pl.BlockSpec((1, HEAD_DIM, N_META), lambda i, g, j: (0, g, 0)),
        ],
        out_specs=pl.BlockSpec((1, group_rows, tq), lambda i, g, j: (i, g, j)),
        out_shape=jax.ShapeDtypeStruct((b, ATTN_WIDTH, n), BF16),
        compiler_params=_params(3),
        name="attention",
    )(qt, k, vt, k_meta, vt_meta)


def _seq_dft_kernel(mc_ref, ms_ref, z_ref, cm_ref, sm_ref, zm_ref, out_ref, acc_ref):
    kk = pl.program_id(2)

    @pl.when(kk == 0)
    def _():
        zm = zm_ref[0]
        acc_ref[...] = (_dot(cm_ref[...], zm[:, :FOURIER_WIDTH]) + _dot(sm_ref[...], zm[:, FOURIER_WIDTH:]))

    z = z_ref[0]
    acc_ref[...] += _dot(mc_ref[...], z[:, :FOURIER_WIDTH]) + _dot(ms_ref[...], z[:, FOURIER_WIDTH:])

    @pl.when(kk == pl.num_programs(2) - 1)
    def _():
        out_ref[0] = acc_ref[...].astype(out_ref.dtype)


def _seq_dft(z, z_meta, mc, ms, cm, sm, tm, tk):
    b, n, _ = z.shape
    return pl.pallas_call(
        _seq_dft_kernel,
        grid=(n // tm, b, n // tk),
        in_specs=[
            pl.BlockSpec((tm, tk), lambda r, i, kk: (r, kk)),
            pl.BlockSpec((tm, tk), lambda r, i, kk: (r, kk)),
            pl.BlockSpec((1, tk, 2 * FOURIER_WIDTH), lambda r, i, kk: (i, kk, 0)),
            pl.BlockSpec((tm, N_META), lambda r, i, kk: (r, 0)),
            pl.BlockSpec((tm, N_META), lambda r, i, kk: (r, 0)),
            pl.BlockSpec((1, N_META, 2 * FOURIER_WIDTH), lambda r, i, kk: (0, 0, 0)),
        ],
        out_specs=pl.BlockSpec((1, tm, FOURIER_WIDTH), lambda r, i, kk: (i, r, 0)),
        out_shape=jax.ShapeDtypeStruct((b, n, FOURIER_WIDTH), BF16),
        scratch_shapes=[pltpu.VMEM((tm, FOURIER_WIDTH), F32)],
        compiler_params=_params(3),
        name="seq_dft",
    )(mc, ms, z, cm, sm, z_meta)


def _fft_a_kernel(e_ref, x_ref, t_ref, xm_ref, out_ref, meta_ref):
    n1p = out_ref.shape[1]
    width = FOURIER_WIDTH

    @pl.when(pl.program_id(1) == 0)
    def _():
        for q in range(FFT_A_RESIDUES_PER_STEP):
            meta_ref[q] = _dot(t_ref[q], xm_ref[...])

    for q in range(FFT_A_RESIDUES_PER_STEP):
        r = _dot(e_ref[q], x_ref[0, :, q * width:(q + 1) * width]) + meta_ref[q]
        for g in range(N_FOURIER_GROUPS):
            base = (q * N_FOURIER_GROUPS + g) * 2 * FOURIER_GROUP
            cols = slice(g * FOURIER_GROUP, (g + 1) * FOURIER_GROUP)
            out_ref[0, :, base:base + FOURIER_GROUP] = r[:n1p, cols].astype(BF16)
            out_ref[0, :, base + FOURIER_GROUP:base + 2 * FOURIER_GROUP] = r[n1p:, cols].astype(BF16)


def _fft_b_kernel(ab_ref, mk_ref, out_ref):
    per_residue = N_FOURIER_GROUPS * 2 * FOURIER_GROUP
    for g in range(N_FOURIER_GROUPS):
        lo = g * 2 * FOURIER_GROUP
        lhs = jnp.concatenate([ab_ref[0, :, q * per_residue + lo:q * per_residue + lo + 2 * FOURIER_GROUP]
                               for q in range(FFT_RADIX)], axis=1)
        res = _dot(lhs, mk_ref[...])
        for k2 in range(FFT_RADIX):
            out_ref[0, k2, :, g * FOURIER_GROUP:(g + 1) * FOURIER_GROUP] = (
                res[:, k2 * FOURIER_GROUP:(k2 + 1) * FOURIER_GROUP].astype(BF16))


def _fft_row_tile(n1p):
    return max(t for t in range(16, 193, 16) if n1p % t == 0)


def _seq_fft(xv, f_meta, e, t, mk):
    b, m, _ = xv.shape
    width = FOURIER_WIDTH
    n1p = e.shape[1] // 2
    qs = FFT_A_RESIDUES_PER_STEP
    per_residue = N_FOURIER_GROUPS * 2 * FOURIER_GROUP
    ab = pl.pallas_call(
        _fft_a_kernel,
        grid=(FFT_RADIX // qs, b),
        in_specs=[
            pl.BlockSpec((qs, 2 * n1p, m), lambda q, i: (q, 0, 0)),
            pl.BlockSpec((1, m, qs * width), lambda q, i: (i, 0, q)),
            pl.BlockSpec((qs, 2 * n1p, FFT_RADIX), lambda q, i: (q, 0, 0)),
            pl.BlockSpec((FFT_RADIX, width), lambda q, i: (0, 0)),
        ],
        out_specs=pl.BlockSpec((1, n1p, qs * per_residue), lambda q, i: (i, 0, q)),
        out_shape=jax.ShapeDtypeStruct((b, n1p, FFT_RADIX * per_residue), BF16),
        scratch_shapes=[pltpu.VMEM((qs, 2 * n1p, width), F32)],
        compiler_params=_params(2),
        name="seq_fft_a",
    )(e, xv, t, f_meta)
    tr = _fft_row_tile(n1p)
    return pl.pallas_call(
        _fft_b_kernel,
        grid=(b, n1p // tr),
        in_specs=[
            pl.BlockSpec((1, tr, FFT_RADIX * per_residue), lambda i, r: (i, r, 0)),
            _const_spec(mk.shape),
        ],
        out_specs=pl.BlockSpec((1, FFT_RADIX, tr, width), lambda i, r: (i, 0, r, 0)),
        out_shape=jax.ShapeDtypeStruct((b, FFT_RADIX, n1p, width), BF16),
        compiler_params=_params(2),
        name="seq_fft_b",
    )(ab, mk)


def _seq_fft_tables(n):
    length = n + N_META
    m = n // FFT_RADIX
    n1 = m + 1
    n1p = -(-n1 // 16) * 16
    k1 = jnp.arange(n1p, dtype=jnp.int32)
    n2 = jnp.arange(FFT_RADIX, dtype=jnp.int32)
    valid = (k1 < n1).astype(F32)
    c1, s1 = _unit_circle(k1[:, None] * (FFT_RADIX * (jnp.arange(m, dtype=jnp.int32) + 1))[None, :], length)
    c2, s2 = _unit_circle(n2[:, None] * k1[None, :], length)
    c2, s2 = c2 * valid[None, :], s2 * valid[None, :]
    cos = c1[None, :, :] * c2[:, :, None] - s1[None, :, :] * s2[:, :, None]
    sin = s1[None, :, :] * c2[:, :, None] + c1[None, :, :] * s2[:, :, None]
    e = jnp.concatenate([cos, -sin], axis=1).astype(BF16)
    pick = (n2[:, None, None] == n2[None, None, :]).astype(F32)
    t = jnp.concatenate([c2[:, :, None] * pick, -s2[:, :, None] * pick], axis=1).astype(BF16)
    ch = jnp.arange(FOURIER_GROUP, dtype=jnp.int32)
    ca, sa = _unit_circle(n2[:, None] * n2[None, :], FFT_RADIX)
    cb, sb = _unit_circle(ch[:, None] * ch[None, :], FOURIER_GROUP)
    ck = ca[:, None, :, None] * cb[None, :, None, :] - sa[:, None, :, None] * sb[None, :, None, :]
    sk = sa[:, None, :, None] * cb[None, :, None, :] + ca[:, None, :, None] * sb[None, :, None, :]
    scale = 1.0 / math.sqrt(length * FOURIER_GROUP)
    mk = (jnp.stack([ck, sk], axis=1) * scale).reshape(FFT_RADIX * 2 * FOURIER_GROUP, FFT_RADIX * FOURIER_GROUP)
    return e, t, mk.astype(BF16)


def _rms(x, g):
    return x * lax.rsqrt(jnp.mean(x * x, axis=-1, keepdims=True) + EPS) * g


def _slab_rows(slab_a_ref, slab_b_ref, tm, n1):
    n1p = slab_a_ref.shape[2]
    first_pos = N_META + pl.program_id(1) * tm
    row0 = first_pos - (first_pos // n1) * n1
    r = lax.broadcasted_iota(jnp.int32, (tm, 2 * n1p), 0) + row0
    col = lax.broadcasted_iota(jnp.int32, (tm, 2 * n1p), 1)
    pick = (col == jnp.where(r < n1, r, r - n1 + n1p)).astype(BF16)
    slabs = jnp.concatenate([slab_a_ref[0, 0], slab_b_ref[0, 0]], axis=0)
    return _dot(pick, slabs).astype(BF16)


def _mix_ffn_kernel(x_ref, ot_ref, gate_ref, *rest, ff_chunk, slab_n1):
    four_refs = rest[:-8]
    wao_ref, wfo_ref, wout_ref, g2_ref, wfi_ref, wfo2_ref, gfin_ref, y_ref = rest[-8:]
    x = x_ref[0]
    d_model = x.shape[-1]
    d_ff = wfo2_ref.shape[0]
    attn = ot_ref[0].astype(F32).T.astype(BF16)
    a = _dot(attn, wao_ref[...])
    four = _slab_rows(*four_refs, x.shape[0], slab_n1) if slab_n1 else four_refs[0][0]
    fo = _dot(four, wfo_ref[...])
    gates = gate_ref[0]
    merged = gates[:, :d_model].astype(F32) * a + gates[:, d_model:].astype(F32) * fo
    x1 = x + _dot(merged.astype(BF16), wout_ref[...])
    h2 = _rms(x1, g2_ref[...]).astype(BF16)
    acc = x1
    for c in range(d_ff // ff_chunk):
        gt = _dot(h2, wfi_ref[:, c * ff_chunk:(c + 1) * ff_chunk])
        up = _dot(h2, wfi_ref[:, d_ff + c * ff_chunk:d_ff + (c + 1) * ff_chunk])
        act = gt * (1.0 / (1.0 + jnp.exp(-gt))) * up
        acc = acc + _dot(act.astype(BF16), wfo2_ref[c * ff_chunk:(c + 1) * ff_chunk, :])
    y_ref[0] = _rms(acc, gfin_ref[...])


def _mix_ffn(x, ot, four, gates, w_attn_o, w_four_o, w_out, g2, w_ffn_in, w_ffn_out, gfin, tm, ff_chunk):
    b, n, d = x.shape
    d_ff = w_ffn_out.shape[0]
    if four.ndim == 4:
        slab_n1 = n // FFT_RADIX + 1
        assert tm <= slab_n1, "a token tile may touch at most two slabs"
        slab_block = (1, 1) + four.shape[2:]
        four_specs = [
            pl.BlockSpec(slab_block, lambda i, j: (i, (N_META + j * tm) // slab_n1, 0, 0)),
            pl.BlockSpec(slab_block,
                         lambda i, j: (i, jnp.minimum((N_META + j * tm) // slab_n1 + 1, FFT_RADIX - 1), 0, 0)),
        ]
        four_args = (four, four)
    else:
        slab_n1 = 0
        four_specs = [pl.BlockSpec((1, tm, FOURIER_WIDTH), lambda i, j: (i, j, 0))]
        four_args = (four,)
    return pl.pallas_call(
        functools.partial(_mix_ffn_kernel, ff_chunk=ff_chunk, slab_n1=slab_n1),
        grid=(b, n // tm),
        in_specs=[
            pl.BlockSpec((1, tm, d), lambda i, j: (i, j, 0)),
            pl.BlockSpec((1, ATTN_WIDTH, tm), lambda i, j: (i, 0, j)),
            pl.BlockSpec((1, tm, 2 * d), lambda i, j: (i, j, 0)),
            *four_specs,
            _const_spec((ATTN_WIDTH, d)),
            _const_spec((FOURIER_WIDTH, d)),
            _const_spec((d, d)),
            _const_spec((1, d)),
            _const_spec((d, 2 * d_ff)),
            _const_spec((d_ff, d)),
            _const_spec((1, d)),
        ],
        out_specs=pl.BlockSpec((1, tm, d), lambda i, j: (i, j, 0)),
        out_shape=jax.ShapeDtypeStruct((b, n, d), F32),
        compiler_params=_params(2),
        name="mix_ffn",
    )(x, ot, gates, *four_args, w_attn_o, w_four_o, w_out, g2, w_ffn_in, w_ffn_out, gfin)


def _rope_tables(row, col):
    inv_freq = 1.0 / (ROPE_THETA ** (jnp.arange(0, ROPE_AXIS_DIM, 2, dtype=F32) / ROPE_AXIS_DIM))
    ar = row[:, None] * inv_freq[None, :]
    ac = col[:, None] * inv_freq[None, :]
    cos = jnp.concatenate([jnp.cos(ar), jnp.cos(ar), jnp.cos(ac), jnp.cos(ac)], axis=-1)
    sin = jnp.concatenate([-jnp.sin(ar), jnp.sin(ar), -jnp.sin(ac), jnp.sin(ac)], axis=-1)
    reps = LANES // HEAD_DIM
    return jnp.tile(cos, (1, reps)), jnp.tile(sin, (1, reps))


def _unit_circle(prod, length):
    ang = (prod % length).astype(F32) * (2.0 * math.pi / length)
    return jnp.cos(ang), jnp.sin(ang)


def _seq_dft_table(n, pos):
    length = n + N_META
    coarse = (jnp.arange(n // GRID_W, dtype=jnp.int32) * GRID_W)[:, None] * pos[None, :]
    fine = (jnp.arange(GRID_W, dtype=jnp.int32) + N_META)[:, None] * pos[None, :]
    c1, s1 = _unit_circle(coarse, length)
    c2, s2 = _unit_circle(fine, length)
    cos = (c1[:, None, :] * c2[None, :, :] - s1[:, None, :] * s2[None, :, :]).reshape(n, pos.shape[0])
    sin = (s1[:, None, :] * c2[None, :, :] + c1[:, None, :] * s2[None, :, :]).reshape(n, pos.shape[0])
    return cos.astype(BF16), sin.astype(BF16)


def _seq_dft_tables(n):
    mc, ms = _seq_dft_table(n, jnp.arange(n, dtype=jnp.int32) + N_META)
    cm, sm = _seq_dft_table(n, jnp.arange(N_META, dtype=jnp.int32))
    return mc, ms, cm, sm


def _channel_dft_table(length):
    idx = jnp.arange(FOURIER_GROUP, dtype=jnp.int32)
    c, s = _unit_circle(idx[:, None] * idx[None, :], FOURIER_GROUP)
    scale = 1.0 / math.sqrt(length * FOURIER_GROUP)
    return (jnp.concatenate([c, -s], axis=1) * scale).astype(BF16)


def _encode(x, meta_tokens, p):
    b, n, d = x.shape
    length = n + N_META
    tok = jnp.arange(n, dtype=jnp.int32)
    cos, sin = _rope_tables((tok // GRID_W).astype(F32), (tok % GRID_W).astype(F32))
    meta_pad = LANES
    mrow = jnp.full((meta_pad,), -1.0, F32)
    mcol = jnp.arange(meta_pad, dtype=F32)
    cos_m, sin_m = _rope_tables(mrow, mcol)
    wc = _channel_dft_table(length)
    x_meta = jnp.zeros((1, meta_pad, d), F32).at[0, :N_META].set(meta_tokens)

    dense_dft = n < FFT_MIN_TOKENS
    proj_consts = (p["norm_mix_g"], p["w_in"])
    head_consts = (p["gq"], p["gk"], p["kbias"], p["hmean"], wc)
    _, k_m, vt_m, z_m, _ = _in_proj(x_meta, *proj_consts, cos_m, sin_m, *head_consts, tm=meta_pad,
                                    fourier="channel_dft" if dense_dft else "raw")
    qt, k, vt, z, gates = _in_proj(x, *proj_consts, cos, sin, *head_consts, tm=min(n, 512),
                                   fourier="channel_dft" if dense_dft else "residue_major")

    attn = functools.partial(_attention, qt, k, vt, k_m[:, :, :N_META], vt_m[:, :, :N_META])
    ot = lax.cond(p["bounded"],
                  functools.partial(attn, tq=min(n, 2 * BOUNDED_Q_SUBTILE), tk=min(n, 256), bounded=True),
                  functools.partial(attn, tq=min(n, 256), tk=min(n, 512), bounded=False))

    if dense_dft:
        mc, ms, cm, sm = _seq_dft_tables(n)
        four = _seq_dft(z, z_m[:, :N_META], mc, ms, cm, sm, tm=min(n, 2048), tk=min(n, 1024))
    else:
        four = _seq_fft(z, z_m[0, :N_META], *_seq_fft_tables(n))

    return _mix_ffn(x, ot, four, gates, p["w_attn_o"], p["w_four_o"], p["w_out"], p["norm_ffn_g"],
                    p["w_ffn_in"], p["w_ffn_out"], p["final_norm_g"], tm=min(n, 256),
                    ff_chunk=p["w_ffn_out"].shape[0])


def kernel(x_prompt, x_sample, meta_tokens, norm_mix_g, w_in, q_norm_g, k_norm_g, w_attn_o, w_four_o, w_out,
           norm_ffn_g, w_ffn_in, w_ffn_out, final_norm_g):
    assert w_in.shape[0] == 1, "single-layer encoder: meta rows are only needed as keys / DFT inputs"
    reps = LANES // HEAD_DIM
    head_id = jnp.arange(2 * LANES) // HEAD_DIM
    bound = 1.01 * LOG2E * HEAD_DIM ** 0.5 * jnp.max(jnp.abs(q_norm_g[0])) * jnp.max(jnp.abs(k_norm_g[0]))
    bounded = bound <= MAX_SCORE_BOUND_LOG2
    p = {
        "bounded": bounded,
        "kbias": jnp.where((jnp.arange(LANES) == HEAD_DIM) & bounded, -bound, 0.0).astype(F32)[None, :],
        "norm_mix_g": norm_mix_g[0][None, :],
        "w_in": w_in[0].astype(BF16),
        "gq": jnp.tile(q_norm_g[0], reps)[None, :],
        "gk": jnp.tile(k_norm_g[0], reps)[None, :],
        "hmean": ((head_id[:, None] == head_id[None, :]).astype(F32) / HEAD_DIM).astype(BF16),
        "w_attn_o": w_attn_o[0].astype(BF16),
        "w_four_o": w_four_o[0].astype(BF16),
        "w_out": w_out[0].astype(BF16),
        "norm_ffn_g": norm_ffn_g[0][None, :],
        "w_ffn_in": w_ffn_in[0].astype(BF16),
        "w_ffn_out": w_ffn_out[0].astype(BF16),
        "final_norm_g": final_norm_g[None, :],
    }
    return (_encode(x_prompt, meta_tokens, p), _encode(x_sample, meta_tokens, p))
```

```python
import functools
import math

import jax
import jax.numpy as jnp
from jax import lax
from jax.experimental import pallas as pl
from jax.experimental.pallas import tpu as pltpu

N_META = 16
GRID_W = 64
HEAD_DIM = 64
N_Q_HEADS = 8
N_KV_HEADS = 2
Q_PER_KV = N_Q_HEADS // N_KV_HEADS
ATTN_WIDTH = N_Q_HEADS * HEAD_DIM
KV_WIDTH = N_KV_HEADS * HEAD_DIM
FOURIER_WIDTH = 512
FOURIER_GROUP = 128
N_FOURIER_GROUPS = FOURIER_WIDTH // FOURIER_GROUP
ROPE_AXIS_DIM = HEAD_DIM // 2
ROPE_THETA = 10000.0
EPS = 1e-6

LANES = 128
VMEM_LIMIT_BYTES = 56 * 1024 * 1024

F32 = jnp.float32
BF16 = jnp.bfloat16

LOG2E = math.log2(math.e)
Q_SCALE = HEAD_DIM ** -0.5 * LOG2E
MAX_SCORE_BOUND_LOG2 = 60.0
BOUNDED_Q_SUBTILE = 2 * LANES

FFT_RADIX = N_META
FFT_A_RESIDUES_PER_STEP = 4
FFT_MIN_TOKENS = 4096


def _dot(a, b):
    return jnp.dot(a, b, preferred_element_type=F32)


def _const_spec(shape):
    zeros = (0,) * len(shape)
    return pl.BlockSpec(shape, lambda *_: zeros, pipeline_mode=pl.Buffered(1))


def _params(n_axes):
    return pltpu.CompilerParams(dimension_semantics=("arbitrary",) * n_axes,
                                vmem_limit_bytes=VMEM_LIMIT_BYTES)


def _in_proj_kernel(x_ref, g_ref, w_ref, cos_ref, sin_ref, gq_ref, gk_ref, kbias_ref, hmean_ref, wc_ref,
                    qt_ref, k_ref, vt_ref, z_ref, gate_ref, *slab_ref):
    x = x_ref[0]
    h = (x * lax.rsqrt(jnp.mean(x * x, axis=-1, keepdims=True) + EPS) * g_ref[...]).astype(BF16)
    cos = cos_ref[...]
    sin = sin_ref[...]
    lane = lax.broadcasted_iota(jnp.int32, cos.shape, 1)
    first_half = (lane % ROPE_AXIS_DIM) < (ROPE_AXIS_DIM // 2)

    def head_norm(z2):
        return z2 * lax.rsqrt(_dot((z2 * z2).astype(BF16), hmean_ref[...]) + EPS)

    def rope(y):
        half = ROPE_AXIS_DIM // 2
        partner = jnp.where(first_half, pltpu.roll(y, LANES - half, 1), pltpu.roll(y, half, 1))
        return y * cos + partner * sin

    o0 = ATTN_WIDTH
    o1 = o0 + KV_WIDTH
    o2 = o1 + KV_WIDTH
    o3 = o2 + FOURIER_WIDTH
    d_model = x.shape[-1]
    tile = 2 * LANES

    zq = _dot(h, w_ref[:, :o0])
    for c in range(ATTN_WIDTH // tile):
        yn = head_norm(zq[:, c * tile:(c + 1) * tile])
        for cc in range(tile // LANES):
            yq = rope(yn[:, cc * LANES:(cc + 1) * LANES] * gq_ref[...]) * Q_SCALE
            r0 = c * tile + cc * LANES
            qt_ref[0, r0:r0 + LANES, :] = yq.T.astype(BF16)

    zkv = _dot(h, w_ref[:, o0:o2])
    yk = rope(head_norm(zkv)[:, :KV_WIDTH] * gk_ref[...])
    low = lane < HEAD_DIM
    kbias = kbias_ref[...]
    k_ref[0, 0] = jnp.where(low, yk, kbias).astype(BF16)
    k_ref[0, 1] = jnp.where(low, pltpu.roll(yk, HEAD_DIM, 1), kbias).astype(BF16)

    vt_ref[0] = zkv[:, KV_WIDTH:].T.astype(BF16)

    zf32 = _dot(h, w_ref[:, o2:o3])
    zf = zf32.astype(BF16)
    if z_ref.shape[-1] == FOURIER_WIDTH:
        z_ref[0] = zf
    elif z_ref.shape[-1] == FFT_RADIX * FOURIER_WIDTH:
        slabs, = slab_ref
        rows = z_ref.shape[1]
        for g in range(N_FOURIER_GROUPS):
            slabs[g] = zf32[:, g * FOURIER_GROUP:(g + 1) * FOURIER_GROUP]
        for j in range(FFT_RADIX):
            for g in range(N_FOURIER_GROUPS):
                lo = j * FOURIER_WIDTH + g * FOURIER_GROUP
                z_ref[0, :, lo:lo + FOURIER_GROUP] = slabs[g, pl.ds(j, rows, stride=FFT_RADIX), :].astype(BF16)
    else:
        for g in range(N_FOURIER_GROUPS):
            f = zf[:, g * FOURIER_GROUP:(g + 1) * FOURIER_GROUP]
            zc = _dot(f, wc_ref[...])
            z_ref[0, :, g * FOURIER_GROUP:(g + 1) * FOURIER_GROUP] = zc[:, :FOURIER_GROUP].astype(BF16)
            z_ref[0, :, FOURIER_WIDTH + g * FOURIER_GROUP:FOURIER_WIDTH + (g + 1) * FOURIER_GROUP] = (
                zc[:, FOURIER_GROUP:].astype(BF16))

    gate_chunk = 512
    for c in range(2 * d_model // gate_chunk):
        zg = _dot(h, w_ref[:, o3 + c * gate_chunk:o3 + (c + 1) * gate_chunk])
        gate_ref[0, :, c * gate_chunk:(c + 1) * gate_chunk] = (1.0 / (1.0 + jnp.exp(-zg))).astype(BF16)


def _in_proj(x, norm_g, w_in, cos, sin, gq, gk, kbias, hmean, wc, tm, fourier):
    b, n, d = x.shape
    in_width = w_in.shape[1]
    grid = (b, n // tm)
    scratch = []
    if fourier == "residue_major":
        four_rows, four_tile, four_width = n // FFT_RADIX, tm // FFT_RADIX, FFT_RADIX * FOURIER_WIDTH
        scratch = [pltpu.VMEM((N_FOURIER_GROUPS, tm, FOURIER_GROUP), F32)]
    else:
        four_rows, four_tile = n, tm
        four_width = 2 * FOURIER_WIDTH if fourier == "channel_dft" else FOURIER_WIDTH
    out_shape = (
        jax.ShapeDtypeStruct((b, ATTN_WIDTH, n), BF16),
        jax.ShapeDtypeStruct((b, N_KV_HEADS, n, LANES), BF16),
        jax.ShapeDtypeStruct((b, KV_WIDTH, n), BF16),
        jax.ShapeDtypeStruct((b, four_rows, four_width), BF16),
        jax.ShapeDtypeStruct((b, n, 2 * d), BF16),
    )
    return pl.pallas_call(
        _in_proj_kernel,
        grid=grid,
        in_specs=[
            pl.BlockSpec((1, tm, d), lambda i, j: (i, j, 0)),
            _const_spec((1, d)),
            _const_spec((d, in_width)),
            pl.BlockSpec((tm, LANES), lambda i, j: (j, 0)),
            pl.BlockSpec((tm, LANES), lambda i, j: (j, 0)),
            _const_spec((1, LANES)),
            _const_spec((1, LANES)),
            _const_spec((1, LANES)),
            _const_spec((2 * LANES, 2 * LANES)),
            _const_spec((FOURIER_GROUP, 2 * FOURIER_GROUP)),
        ],
        out_specs=(
            pl.BlockSpec((1, ATTN_WIDTH, tm), lambda i, j: (i, 0, j)),
            pl.BlockSpec((1, N_KV_HEADS, tm, LANES), lambda i, j: (i, 0, j, 0)),
            pl.BlockSpec((1, KV_WIDTH, tm), lambda i, j: (i, 0, j)),
            pl.BlockSpec((1, four_tile, four_width), lambda i, j: (i, j, 0)),
            pl.BlockSpec((1, tm, 2 * d), lambda i, j: (i, j, 0)),
        ),
        out_shape=out_shape,
        scratch_shapes=scratch,
        compiler_params=_params(2),
        name="in_proj",
    )(x, norm_g, w_in, cos, sin, gq, gk, kbias, hmean, wc)


def _attn_kernel(qt_ref, k_ref, vt_ref, km_ref, vtm_ref, ot_ref, *, tk, bounded):
    tq = qt_ref.shape[2]
    n_keys = k_ref.shape[2]
    n_blocks = n_keys // tk
    heads = range(Q_PER_KV)

    def head_qt(h, lo, width):
        ones_row = (lax.broadcasted_iota(jnp.int32, (HEAD_DIM, width), 0) == 0).astype(BF16)
        return jnp.concatenate([qt_ref[0, h * HEAD_DIM:(h + 1) * HEAD_DIM, lo:lo + width], ones_row], axis=0)

    if bounded:
        sub = min(tq, BOUNDED_Q_SUBTILE)
        for lo in range(0, tq, sub):
            qts = [head_qt(h, lo, sub) for h in heads]
            p = [jnp.exp2(_dot(km_ref[0, 0], qts[h])) for h in heads]
            l = [jnp.sum(p[h], axis=0, keepdims=True) for h in heads]
            acc = [_dot(vtm_ref[0], p[h].astype(BF16)) for h in heads]
            s = [_dot(k_ref[0, 0, 0:tk, :], qts[h]) for h in heads]
            for j in range(n_blocks):
                for h in heads:
                    ph = jnp.exp2(s[h])
                    l[h] = l[h] + jnp.sum(ph, axis=0, keepdims=True)
                    if j + 1 < n_blocks:
                        s[h] = _dot(k_ref[0, 0, (j + 1) * tk:(j + 2) * tk, :], qts[h])
                    acc[h] = acc[h] + _dot(vt_ref[0, :, j * tk:(j + 1) * tk], ph.astype(BF16))
            for h in heads:
                ot_ref[0, h * HEAD_DIM:(h + 1) * HEAD_DIM, lo:lo + sub] = (acc[h] / l[h]).astype(BF16)
        return

    qt = jnp.concatenate([head_qt(h, 0, tq) for h in heads], axis=1)

    def block(kb, vb, m, l, acc):
        s = _dot(kb, qt)
        m_new = jnp.maximum(m, jnp.max(s, axis=0, keepdims=True))
        alpha = jnp.exp2(m - m_new)
        p = jnp.exp2(s - m_new)
        return (m_new, alpha * l + jnp.sum(p, axis=0, keepdims=True), alpha * acc + _dot(vb, p.astype(BF16)))

    width = qt.shape[1]
    init = (jnp.full((1, width), -jnp.inf, F32), jnp.zeros((1, width), F32), jnp.zeros((HEAD_DIM, width), F32))
    carry = block(km_ref[0, 0], vtm_ref[0], *init)

    def body(j, carry):
        start = pl.multiple_of(j * tk, tk)
        return block(k_ref[0, 0, pl.ds(start, tk), :], vt_ref[0, :, pl.ds(start, tk)], *carry)

    _, l, acc = lax.fori_loop(0, n_blocks, body, carry)
    out = (acc / l).astype(BF16)
    for h in range(Q_PER_KV):
        ot_ref[0, h * HEAD_DIM:(h + 1) * HEAD_DIM, :] = out[:, h * tq:(h + 1) * tq]


def _attention(qt, k, vt, k_meta, vt_meta, tq, tk, bounded):
    b, _, n = qt.shape
    group_rows = Q_PER_KV * HEAD_DIM
    return pl.pallas_call(
        functools.partial(_attn_kernel, tk=tk, bounded=bounded),
        grid=(b, N_KV_HEADS, n // tq),
        in_specs=[
            pl.BlockSpec((1, group_rows, tq), lambda i, g, j: (i, g, j)),
            pl.BlockSpec((1, 1, n, LANES), lambda i, g, j: (i, g, 0, 0)),
            pl.BlockSpec((1, HEAD_DIM, n), lambda i, g, j: (i, g, 0)),
            pl.BlockSpec((1, 1, N_META, LANES), lambda i, g, j: (0, g, 0, 0)),
            pl.BlockSpec((1, HEAD_DIM, N_META), lambda i, g, j: (0, g, 0)),
        ],
        out_specs=pl.BlockSpec((1, group_rows, tq), lambda i, g, j: (i, g, j)),
        out_shape=jax.ShapeDtypeStruct((b, ATTN_WIDTH, n), BF16),
        compiler_params=_params(3),
        name="attention",
    )(qt, k, vt, k_meta, vt_meta)


def _seq_dft_kernel(mc_ref, ms_ref, z_ref, cm_ref, sm_ref, zm_ref, out_ref, acc_ref):
    kk = pl.program_id(2)

    @pl.when(kk == 0)
    def _():
        zm = zm_ref[0]
        acc_ref[...] = (_dot(cm_ref[...], zm[:, :FOURIER_WIDTH]) + _dot(sm_ref[...], zm[:, FOURIER_WIDTH:]))

    z = z_ref[0]
    acc_ref[...] += _dot(mc_ref[...], z[:, :FOURIER_WIDTH]) + _dot(ms_ref[...], z[:, FOURIER_WIDTH:])

    @pl.when(kk == pl.num_programs(2) - 1)
    def _():
        out_ref[0] = acc_ref[...].astype(out_ref.dtype)


def _seq_dft(z, z_meta, mc, ms, cm, sm, tm, tk):
    b, n, _ = z.shape
    return pl.pallas_call(
        _seq_dft_kernel,
        grid=(n // tm, b, n // tk),
        in_specs=[
            pl.BlockSpec((tm, tk), lambda r, i, kk: (r, kk)),
            pl.BlockSpec((tm, tk), lambda r, i, kk: (r, kk)),
            pl.BlockSpec((1, tk, 2 * FOURIER_WIDTH), lambda r, i, kk: (i, kk, 0)),
            pl.BlockSpec((tm, N_META), lambda r, i, kk: (r, 0)),
            pl.BlockSpec((tm, N_META), lambda r, i, kk: (r, 0)),
            pl.BlockSpec((1, N_META, 2 * FOURIER_WIDTH), lambda r, i, kk: (0, 0, 0)),
        ],
        out_specs=pl.BlockSpec((1, tm, FOURIER_WIDTH), lambda r, i, kk: (i, r, 0)),
        out_shape=jax.ShapeDtypeStruct((b, n, FOURIER_WIDTH), BF16),
        scratch_shapes=[pltpu.VMEM((tm, FOURIER_WIDTH), F32)],
        compiler_params=_params(3),
        name="seq_dft",
    )(mc, ms, z, cm, sm, z_meta)


def _fft_a_kernel(e_ref, x_ref, t_ref, xm_ref, out_ref, meta_ref):
    n1p = out_ref.shape[1]
    width = FOURIER_WIDTH

    @pl.when(pl.program_id(1) == 0)
    def _():
        for q in range(FFT_A_RESIDUES_PER_STEP):
            meta_ref[q] = _dot(t_ref[q], xm_ref[...])

    for q in range(FFT_A_RESIDUES_PER_STEP):
        r = _dot(e_ref[q], x_ref[0, :, q * width:(q + 1) * width]) + meta_ref[q]
        for g in range(N_FOURIER_GROUPS):
            base = (q * N_FOURIER_GROUPS + g) * 2 * FOURIER_GROUP
            cols = slice(g * FOURIER_GROUP, (g + 1) * FOURIER_GROUP)
            out_ref[0, :, base:base + FOURIER_GROUP] = r[:n1p, cols].astype(BF16)
            out_ref[0, :, base + FOURIER_GROUP:base + 2 * FOURIER_GROUP] = r[n1p:, cols].astype(BF16)


def _fft_b_kernel(ab_ref, mk_ref, out_ref):
    per_residue = N_FOURIER_GROUPS * 2 * FOURIER_GROUP
    for g in range(N_FOURIER_GROUPS):
        lo = g * 2 * FOURIER_GROUP
        lhs = jnp.concatenate([ab_ref[0, :, q * per_residue + lo:q * per_residue + lo + 2 * FOURIER_GROUP]
                               for q in range(FFT_RADIX)], axis=1)
        res = _dot(lhs, mk_ref[...])
        for k2 in range(FFT_RADIX):
            out_ref[0, k2, :, g * FOURIER_GROUP:(g + 1) * FOURIER_GROUP] = (
                res[:, k2 * FOURIER_GROUP:(k2 + 1) * FOURIER_GROUP].astype(BF16))


def _fft_row_tile(n1p):
    return max(t for t in range(16, 193, 16) if n1p % t == 0)


def _seq_fft(xv, f_meta, e, t, mk):
    b, m, _ = xv.shape
    width = FOURIER_WIDTH
    n1p = e.shape[1] // 2
    qs = FFT_A_RESIDUES_PER_STEP
    per_residue = N_FOURIER_GROUPS * 2 * FOURIER_GROUP
    ab = pl.pallas_call(
        _fft_a_kernel,
        grid=(FFT_RADIX // qs, b),
        in_specs=[
            pl.BlockSpec((qs, 2 * n1p, m), lambda q, i: (q, 0, 0)),
            pl.BlockSpec((1, m, qs * width), lambda q, i: (i, 0, q)),
            pl.BlockSpec((qs, 2 * n1p, FFT_RADIX), lambda q, i: (q, 0, 0)),
            pl.BlockSpec((FFT_RADIX, width), lambda q, i: (0, 0)),
        ],
        out_specs=pl.BlockSpec((1, n1p, qs * per_residue), lambda q, i: (i, 0, q)),
        out_shape=jax.ShapeDtypeStruct((b, n1p, FFT_RADIX * per_residue), BF16),
        scratch_shapes=[pltpu.VMEM((qs, 2 * n1p, width), F32)],
        compiler_params=_params(2),
        name="seq_fft_a",
    )(e, xv, t, f_meta)
    tr = _fft_row_tile(n1p)
    return pl.pallas_call(
        _fft_b_kernel,
        grid=(b, n1p // tr),
        in_specs=[
            pl.BlockSpec((1, tr, FFT_RADIX * per_residue), lambda i, r: (i, r, 0)),
            _const_spec(mk.shape),
        ],
        out_specs=pl.BlockSpec((1, FFT_RADIX, tr, width), lambda i, r: (i, 0, r, 0)),
        out_shape=jax.ShapeDtypeStruct((b, FFT_RADIX, n1p, width), BF16),
        compiler_params=_params(2),
        name="seq_fft_b",
    )(ab, mk)


def _seq_fft_tables(n):
    length = n + N_META
    m = n // FFT_RADIX
    n1 = m + 1
    n1p = -(-n1 // 16) * 16
    k1 = jnp.arange(n1p, dtype=jnp.int32)
    n2 = jnp.arange(FFT_RADIX, dtype=jnp.int32)
    valid = (k1 < n1).astype(F32)
    c1, s1 = _unit_circle(k1[:, None] * (FFT_RADIX * (jnp.arange(m, dtype=jnp.int32) + 1))[None, :], length)
    c2, s2 = _unit_circle(n2[:, None] * k1[None, :], length)
    c2, s2 = c2 * valid[None, :], s2 * valid[None, :]
    cos = c1[None, :, :] * c2[:, :, None] - s1[None, :, :] * s2[:, :, None]
    sin = s1[None, :, :] * c2[:, :, None] + c1[None, :, :] * s2[:, :, None]
    e = jnp.concatenate([cos, -sin], axis=1).astype(BF16)
    pick = (n2[:, None, None] == n2[None, None, :]).astype(F32)
    t = jnp.concatenate([c2[:, :, None] * pick, -s2[:, :, None] * pick], axis=1).astype(BF16)
    ch = jnp.arange(FOURIER_GROUP, dtype=jnp.int32)
    ca, sa = _unit_circle(n2[:, None] * n2[None, :], FFT_RADIX)
    cb, sb = _unit_circle(ch[:, None] * ch[None, :], FOURIER_GROUP)
    ck = ca[:, None, :, None] * cb[None, :, None, :] - sa[:, None, :, None] * sb[None, :, None, :]
    sk = sa[:, None, :, None] * cb[None, :, None, :] + ca[:, None, :, None] * sb[None, :, None, :]
    scale = 1.0 / math.sqrt(length * FOURIER_GROUP)
    mk = (jnp.stack([ck, sk], axis=1) * scale).reshape(FFT_RADIX * 2 * FOURIER_GROUP, FFT_RADIX * FOURIER_GROUP)
    return e, t, mk.astype(BF16)


def _rms(x, g):
    return x * lax.rsqrt(jnp.mean(x * x, axis=-1, keepdims=True) + EPS) * g


def _slab_rows(slab_a_ref, slab_b_ref, tm, n1):
    n1p = slab_a_ref.shape[2]
    first_pos = N_META + pl.program_id(1) * tm
    row0 = first_pos - (first_pos // n1) * n1
    r = lax.broadcasted_iota(jnp.int32, (tm, 2 * n1p), 0) + row0
    col = lax.broadcasted_iota(jnp.int32, (tm, 2 * n1p), 1)
    pick = (col == jnp.where(r < n1, r, r - n1 + n1p)).astype(BF16)
    slabs = jnp.concatenate([slab_a_ref[0, 0], slab_b_ref[0, 0]], axis=0)
    return _dot(pick, slabs).astype(BF16)


def _mix_ffn_kernel(x_ref, ot_ref, gate_ref, *rest, ff_chunk, slab_n1):
    four_refs = rest[:-8]
    wao_ref, wfo_ref, wout_ref, g2_ref, wfi_ref, wfo2_ref, gfin_ref, y_ref = rest[-8:]
    x = x_ref[0]
    d_model = x.shape[-1]
    d_ff = wfo2_ref.shape[0]
    attn = ot_ref[0].astype(F32).T.astype(BF16)
    a = _dot(attn, wao_ref[...])
    four = _slab_rows(*four_refs, x.shape[0], slab_n1) if slab_n1 else four_refs[0][0]
    fo = _dot(four, wfo_ref[...])
    gates = gate_ref[0]
    merged = gates[:, :d_model].astype(F32) * a + gates[:, d_model:].astype(F32) * fo
    x1 = x + _dot(merged.astype(BF16), wout_ref[...])
    h2 = _rms(x1, g2_ref[...]).astype(BF16)
    acc = x1
    for c in range(d_ff // ff_chunk):
        gt = _dot(h2, wfi_ref[:, c * ff_chunk:(c + 1) * ff_chunk])
        up = _dot(h2, wfi_ref[:, d_ff + c * ff_chunk:d_ff + (c + 1) * ff_chunk])
        act = gt * (1.0 / (1.0 + jnp.exp(-gt))) * up
        acc = acc + _dot(act.astype(BF16), wfo2_ref[c * ff_chunk:(c + 1) * ff_chunk, :])
    y_ref[0] = _rms(acc, gfin_ref[...])


def _mix_ffn(x, ot, four, gates, w_attn_o, w_four_o, w_out, g2, w_ffn_in, w_ffn_out, gfin, tm, ff_chunk):
    b, n, d = x.shape
    d_ff = w_ffn_out.shape[0]
    if four.ndim == 4:
        slab_n1 = n // FFT_RADIX + 1
        assert tm <= slab_n1, "a token tile may touch at most two slabs"
        slab_block = (1, 1) + four.shape[2:]
        four_specs = [
            pl.BlockSpec(slab_block, lambda i, j: (i, (N_META + j * tm) // slab_n1, 0, 0)),
            pl.BlockSpec(slab_block,
                         lambda i, j: (i, jnp.minimum((N_META + j * tm) // slab_n1 + 1, FFT_RADIX - 1), 0, 0)),
        ]
        four_args = (four, four)
    else:
        slab_n1 = 0
        four_specs = [pl.BlockSpec((1, tm, FOURIER_WIDTH), lambda i, j: (i, j, 0))]
        four_args = (four,)
    return pl.pallas_call(
        functools.partial(_mix_ffn_kernel, ff_chunk=ff_chunk, slab_n1=slab_n1),
        grid=(b, n // tm),
        in_specs=[
            pl.BlockSpec((1, tm, d), lambda i, j: (i, j, 0)),
            pl.BlockSpec((1, ATTN_WIDTH, tm), lambda i, j: (i, 0, j)),
            pl.BlockSpec((1, tm, 2 * d), lambda i, j: (i, j, 0)),
            *four_specs,
            _const_spec((ATTN_WIDTH, d)),
            _const_spec((FOURIER_WIDTH, d)),
            _const_spec((d, d)),
            _const_spec((1, d)),
            _const_spec((d, 2 * d_ff)),
            _const_spec((d_ff, d)),
            _const_spec((1, d)),
        ],
        out_specs=pl.BlockSpec((1, tm, d), lambda i, j: (i, j, 0)),
        out_shape=jax.ShapeDtypeStruct((b, n, d), F32),
        compiler_params=_params(2),
        name="mix_ffn",
    )(x, ot, gates, *four_args, w_attn_o, w_four_o, w_out, g2, w_ffn_in, w_ffn_out, gfin)


def _rope_tables(row, col):
    inv_freq = 1.0 / (ROPE_THETA ** (jnp.arange(0, ROPE_AXIS_DIM, 2, dtype=F32) / ROPE_AXIS_DIM))
    ar = row[:, None] * inv_freq[None, :]
    ac = col[:, None] * inv_freq[None, :]
    cos = jnp.concatenate([jnp.cos(ar), jnp.cos(ar), jnp.cos(ac), jnp.cos(ac)], axis=-1)
    sin = jnp.concatenate([-jnp.sin(ar), jnp.sin(ar), -jnp.sin(ac), jnp.sin(ac)], axis=-1)
    reps = LANES // HEAD_DIM
    return jnp.tile(cos, (1, reps)), jnp.tile(sin, (1, reps))


def _unit_circle(prod, length):
    ang = (prod % length).astype(F32) * (2.0 * math.pi / length)
    return jnp.cos(ang), jnp.sin(ang)


def _seq_dft_table(n, pos):
    length = n + N_META
    coarse = (jnp.arange(n // GRID_W, dtype=jnp.int32) * GRID_W)[:, None] * pos[None, :]
    fine = (jnp.arange(GRID_W, dtype=jnp.int32) + N_META)[:, None] * pos[None, :]
    c1, s1 = _unit_circle(coarse, length)
    c2, s2 = _unit_circle(fine, length)
    cos = (c1[:, None, :] * c2[None, :, :] - s1[:, None, :] * s2[None, :, :]).reshape(n, pos.shape[0])
    sin = (s1[:, None, :] * c2[None, :, :] + c1[:, None, :] * s2[None, :, :]).reshape(n, pos.shape[0])
    return cos.astype(BF16), sin.astype(BF16)


def _seq_dft_tables(n):
    mc, ms = _seq_dft_table(n, jnp.arange(n, dtype=jnp.int32) + N_META)
    cm, sm = _seq_dft_table(n, jnp.arange(N_META, dtype=jnp.int32))
    return mc, ms, cm, sm


def _channel_dft_table(length):
    idx = jnp.arange(FOURIER_GROUP, dtype=jnp.int32)
    c, s = _unit_circle(idx[:, None] * idx[None, :], FOURIER_GROUP)
    scale = 1.0 / math.sqrt(length * FOURIER_GROUP)
    return (jnp.concatenate([c, -s], axis=1) * scale).astype(BF16)


def _encode(x, meta_tokens, p):
    b, n, d = x.shape
    length = n + N_META
    tok = jnp.arange(n, dtype=jnp.int32)
    cos, sin = _rope_tables((tok // GRID_W).astype(F32), (tok % GRID_W).astype(F32))
    meta_pad = LANES
    mrow = jnp.full((meta_pad,), -1.0, F32)
    mcol = jnp.arange(meta_pad, dtype=F32)
    cos_m, sin_m = _rope_tables(mrow, mcol)
    wc = _channel_dft_table(length)
    x_meta = jnp.zeros((1, meta_pad, d), F32).at[0, :N_META].set(meta_tokens)

    dense_dft = n < FFT_MIN_TOKENS
    proj_consts = (p["norm_mix_g"], p["w_in"])
    head_consts = (p["gq"], p["gk"], p["kbias"], p["hmean"], wc)
    _, k_m, vt_m, z_m, _ = _in_proj(x_meta, *proj_consts, cos_m, sin_m, *head_consts, tm=meta_pad,
                                    fourier="channel_dft" if dense_dft else "raw")
    qt, k, vt, z, gates = _in_proj(x, *proj_consts, cos, sin, *head_consts, tm=min(n, 1024 if dense_dft else 512),
                                   fourier="channel_dft" if dense_dft else "residue_major")

    attn = functools.partial(_attention, qt, k, vt, k_m[:, :, :N_META], vt_m[:, :, :N_META])
    ot = lax.cond(p["bounded"],
                  functools.partial(attn, tq=min(n, 2 * BOUNDED_Q_SUBTILE), tk=min(n, 256), bounded=True),
                  functools.partial(attn, tq=min(n, 256), tk=min(n, 512), bounded=False))

    if dense_dft:
        mc, ms, cm, sm = _seq_dft_tables(n)
        four = _seq_dft(z, z_m[:, :N_META], mc, ms, cm, sm, tm=min(n, 2048), tk=min(n, 1024))
    else:
        four = _seq_fft(z, z_m[0, :N_META], *_seq_fft_tables(n))

    return _mix_ffn(x, ot, four, gates, p["w_attn_o"], p["w_four_o"], p["w_out"], p["norm_ffn_g"],
                    p["w_ffn_in"], p["w_ffn_out"], p["final_norm_g"], tm=min(n, 512),
                    ff_chunk=p["w_ffn_out"].shape[0])


def kernel(x_prompt, x_sample, meta_tokens, norm_mix_g, w_in, q_norm_g, k_norm_g, w_attn_o, w_four_o, w_out,
           norm_ffn_g, w_ffn_in, w_ffn_out, final_norm_g):
    assert w_in.shape[0] == 1, "single-layer encoder: meta rows are only needed as keys / DFT inputs"
    reps = LANES // HEAD_DIM
    head_id = jnp.arange(2 * LANES) // HEAD_DIM
    bound = 1.01 * LOG2E * HEAD_DIM ** 0.5 * jnp.max(jnp.abs(q_norm_g[0])) * jnp.max(jnp.abs(k_norm_g[0]))
    bounded = bound <= MAX_SCORE_BOUND_LOG2
    p = {
        "bounded": bounded,
        "kbias": jnp.where((jnp.arange(LANES) == HEAD_DIM) & bounded, -bound, 0.0).astype(F32)[None, :],
        "norm_mix_g": norm_mix_g[0][None, :],
        "w_in": w_in[0].astype(BF16),
        "gq": jnp.tile(q_norm_g[0], reps)[None, :],
        "gk": jnp.tile(k_norm_g[0], reps)[None, :],
        "hmean": ((head_id[:, None] == head_id[None, :]).astype(F32) / HEAD_DIM).astype(BF16),
        "w_attn_o": w_attn_o[0].astype(BF16),
        "w_four_o": w_four_o[0].astype(BF16),
        "w_out": w_out[0].astype(BF16),
        "norm_ffn_g": norm_ffn_g[0][None, :],
        "w_ffn_in": w_ffn_in[0].astype(BF16),
        "w_ffn_out": w_ffn_out[0].astype(BF16),
        "final_norm_g": final_norm_g[None, :],
    }
    return (_encode(x_prompt, meta_tokens, p), _encode(x_sample, meta_tokens, p))
```

```python
import functools
import math

import jax
import jax.numpy as jnp
from jax import lax
from jax.experimental import pallas as pl
from jax.experimental.pallas import tpu as pltpu

N_META = 16
GRID_W = 64
HEAD_DIM = 64
N_Q_HEADS = 8
N_KV_HEADS = 2
Q_PER_KV = N_Q_HEADS // N_KV_HEADS
ATTN_WIDTH = N_Q_HEADS * HEAD_DIM
KV_WIDTH = N_KV_HEADS * HEAD_DIM
FOURIER_WIDTH = 512
FOURIER_GROUP = 128
N_FOURIER_GROUPS = FOURIER_WIDTH // FOURIER_GROUP
ROPE_AXIS_DIM = HEAD_DIM // 2
ROPE_THETA = 10000.0
EPS = 1e-6

LANES = 128
VMEM_LIMIT_BYTES = 56 * 1024 * 1024

F32 = jnp.float32
BF16 = jnp.bfloat16

LOG2E = math.log2(math.e)
Q_SCALE = HEAD_DIM ** -0.5 * LOG2E
MAX_SCORE_BOUND_LOG2 = 60.0
BOUNDED_Q_SUBTILE = 2 * LANES

FFT_RADIX = N_META
FFT_A_RESIDUES_PER_STEP = 4
FFT_A_SMALL_MATRIX = 1 << 16

IN_PROJ_TILE = 512
MIX_FFN_TILE = 512


def _dot(a, b):
    return jnp.dot(a, b, preferred_element_type=F32)


def _const_spec(shape):
    zeros = (0,) * len(shape)
    return pl.BlockSpec(shape, lambda *_: zeros, pipeline_mode=pl.Buffered(1))


def _params(n_axes):
    return pltpu.CompilerParams(dimension_semantics=("arbitrary",) * n_axes,
                                vmem_limit_bytes=VMEM_LIMIT_BYTES)


def _in_proj_kernel(x_ref, g_ref, w_ref, cos_ref, sin_ref, gq_ref, gk_ref, kbias_ref, hmean_ref,
                    qt_ref, k_ref, vt_ref, z_ref, gate_ref, *slab_ref):
    x = x_ref[0]
    h = (x * lax.rsqrt(jnp.mean(x * x, axis=-1, keepdims=True) + EPS) * g_ref[...]).astype(BF16)
    cos = cos_ref[...]
    sin = sin_ref[...]
    lane = lax.broadcasted_iota(jnp.int32, cos.shape, 1)
    first_half = (lane % ROPE_AXIS_DIM) < (ROPE_AXIS_DIM // 2)

    def head_norm(z2):
        return z2 * lax.rsqrt(_dot((z2 * z2).astype(BF16), hmean_ref[...]) + EPS)

    def rope(y):
        half = ROPE_AXIS_DIM // 2
        partner = jnp.where(first_half, pltpu.roll(y, LANES - half, 1), pltpu.roll(y, half, 1))
        return y * cos + partner * sin

    o0 = ATTN_WIDTH
    o1 = o0 + KV_WIDTH
    o2 = o1 + KV_WIDTH
    o3 = o2 + FOURIER_WIDTH
    d_model = x.shape[-1]
    tile = 2 * LANES

    zq = _dot(h, w_ref[:, :o0])
    for c in range(ATTN_WIDTH // tile):
        yn = head_norm(zq[:, c * tile:(c + 1) * tile])
        for cc in range(tile // LANES):
            yq = rope(yn[:, cc * LANES:(cc + 1) * LANES] * gq_ref[...]) * Q_SCALE
            r0 = c * tile + cc * LANES
            qt_ref[0, r0:r0 + LANES, :] = yq.T.astype(BF16)

    zkv = _dot(h, w_ref[:, o0:o2])
    yk = rope(head_norm(zkv)[:, :KV_WIDTH] * gk_ref[...])
    low = lane < HEAD_DIM
    kbias = kbias_ref[...]
    k_ref[0, 0] = jnp.where(low, yk, kbias).astype(BF16)
    k_ref[0, 1] = jnp.where(low, pltpu.roll(yk, HEAD_DIM, 1), kbias).astype(BF16)

    vt_ref[0] = zkv[:, KV_WIDTH:].T.astype(BF16)

    zf = _dot(h, w_ref[:, o2:o3])
    if not slab_ref:
        z_ref[0] = zf.astype(BF16)
    else:
        slabs, = slab_ref
        rows = z_ref.shape[1]
        for g in range(N_FOURIER_GROUPS):
            slabs[g] = zf[:, g * FOURIER_GROUP:(g + 1) * FOURIER_GROUP]
        for j in range(FFT_RADIX):
            for g in range(N_FOURIER_GROUPS):
                lo = j * FOURIER_WIDTH + g * FOURIER_GROUP
                z_ref[0, :, lo:lo + FOURIER_GROUP] = slabs[g, pl.ds(j, rows, stride=FFT_RADIX), :].astype(BF16)

    gate_chunk = 512
    for c in range(2 * d_model // gate_chunk):
        zg = _dot(h, w_ref[:, o3 + c * gate_chunk:o3 + (c + 1) * gate_chunk])
        gate_ref[0, :, c * gate_chunk:(c + 1) * gate_chunk] = (1.0 / (1.0 + jnp.exp(-zg))).astype(BF16)


def _in_proj(x, norm_g, w_in, cos, sin, gq, gk, kbias, hmean, tm, residue_major):
    b, n, d = x.shape
    in_width = w_in.shape[1]
    grid = (b, n // tm)
    if residue_major:
        four_rows, four_tile, four_width = n // FFT_RADIX, tm // FFT_RADIX, FFT_RADIX * FOURIER_WIDTH
        scratch = [pltpu.VMEM((N_FOURIER_GROUPS, tm, FOURIER_GROUP), F32)]
    else:
        four_rows, four_tile, four_width = n, tm, FOURIER_WIDTH
        scratch = []
    out_shape = (
        jax.ShapeDtypeStruct((b, ATTN_WIDTH, n), BF16),
        jax.ShapeDtypeStruct((b, N_KV_HEADS, n, LANES), BF16),
        jax.ShapeDtypeStruct((b, KV_WIDTH, n), BF16),
        jax.ShapeDtypeStruct((b, four_rows, four_width), BF16),
        jax.ShapeDtypeStruct((b, n, 2 * d), BF16),
    )
    return pl.pallas_call(
        _in_proj_kernel,
        grid=grid,
        in_specs=[
            pl.BlockSpec((1, tm, d), lambda i, j: (i, j, 0)),
            _const_spec((1, d)),
            _const_spec((d, in_width)),
            pl.BlockSpec((tm, LANES), lambda i, j: (j, 0)),
            pl.BlockSpec((tm, LANES), lambda i, j: (j, 0)),
            _const_spec((1, LANES)),
            _const_spec((1, LANES)),
            _const_spec((1, LANES)),
            _const_spec((2 * LANES, 2 * LANES)),
        ],
        out_specs=(
            pl.BlockSpec((1, ATTN_WIDTH, tm), lambda i, j: (i, 0, j)),
            pl.BlockSpec((1, N_KV_HEADS, tm, LANES), lambda i, j: (i, 0, j, 0)),
            pl.BlockSpec((1, KV_WIDTH, tm), lambda i, j: (i, 0, j)),
            pl.BlockSpec((1, four_tile, four_width), lambda i, j: (i, j, 0)),
            pl.BlockSpec((1, tm, 2 * d), lambda i, j: (i, j, 0)),
        ),
        out_shape=out_shape,
        scratch_shapes=scratch,
        compiler_params=_params(2),
        name="in_proj",
    )(x, norm_g, w_in, cos, sin, gq, gk, kbias, hmean)


def _attn_kernel(qt_ref, k_ref, vt_ref, km_ref, vtm_ref, ot_ref, *, tk, bounded):
    tq = qt_ref.shape[2]
    n_keys = k_ref.shape[2]
    n_blocks = n_keys // tk
    heads = range(Q_PER_KV)

    def head_qt(h, lo, width):
        ones_row = (lax.broadcasted_iota(jnp.int32, (HEAD_DIM, width), 0) == 0).astype(BF16)
        return jnp.concatenate([qt_ref[0, h * HEAD_DIM:(h + 1) * HEAD_DIM, lo:lo + width], ones_row], axis=0)

    if bounded:
        sub = min(tq, BOUNDED_Q_SUBTILE)
        for lo in range(0, tq, sub):
            qts = [head_qt(h, lo, sub) for h in heads]
            p = [jnp.exp2(_dot(km_ref[0, 0], qts[h])) for h in heads]
            l = [jnp.sum(p[h], axis=0, keepdims=True) for h in heads]
            acc = [_dot(vtm_ref[0], p[h].astype(BF16)) for h in heads]
            s = [_dot(k_ref[0, 0, 0:tk, :], qts[h]) for h in heads]
            for j in range(n_blocks):
                for h in heads:
                    ph = jnp.exp2(s[h])
                    l[h] = l[h] + jnp.sum(ph, axis=0, keepdims=True)
                    if j + 1 < n_blocks:
                        s[h] = _dot(k_ref[0, 0, (j + 1) * tk:(j + 2) * tk, :], qts[h])
                    acc[h] = acc[h] + _dot(vt_ref[0, :, j * tk:(j + 1) * tk], ph.astype(BF16))
            for h in heads:
                ot_ref[0, h * HEAD_DIM:(h + 1) * HEAD_DIM, lo:lo + sub] = (acc[h] / l[h]).astype(BF16)
        return

    qt = jnp.concatenate([head_qt(h, 0, tq) for h in heads], axis=1)

    def block(kb, vb, m, l, acc):
        s = _dot(kb, qt)
        m_new = jnp.maximum(m, jnp.max(s, axis=0, keepdims=True))
        alpha = jnp.exp2(m - m_new)
        p = jnp.exp2(s - m_new)
        return (m_new, alpha * l + jnp.sum(p, axis=0, keepdims=True), alpha * acc + _dot(vb, p.astype(BF16)))

    width = qt.shape[1]
    init = (jnp.full((1, width), -jnp.inf, F32), jnp.zeros((1, width), F32), jnp.zeros((HEAD_DIM, width), F32))
    carry = block(km_ref[0, 0], vtm_ref[0], *init)

    def body(j, carry):
        start = pl.multiple_of(j * tk, tk)
        return block(k_ref[0, 0, pl.ds(start, tk), :], vt_ref[0, :, pl.ds(start, tk)], *carry)

    _, l, acc = lax.fori_loop(0, n_blocks, body, carry)
    out = (acc / l).astype(BF16)
    for h in range(Q_PER_KV):
        ot_ref[0, h * HEAD_DIM:(h + 1) * HEAD_DIM, :] = out[:, h * tq:(h + 1) * tq]


def _attention(qt, k, vt, k_meta, vt_meta, tq, tk, bounded):
    b, _, n = qt.shape
    group_rows = Q_PER_KV * HEAD_DIM
    return pl.pallas_call(
        functools.partial(_attn_kernel, tk=tk, bounded=bounded),
        grid=(b, N_KV_HEADS, n // tq),
        in_specs=[
            pl.BlockSpec((1, group_rows, tq), lambda i, g, j: (i, g, j)),
            pl.BlockSpec((1, 1, n, LANES), lambda i, g, j: (i, g, 0, 0)),
            pl.BlockSpec((1, HEAD_DIM, n), lambda i, g, j: (i, g, 0)),
            pl.BlockSpec((1, 1, N_META, LANES), lambda i, g, j: (0, g, 0, 0)),
            pl.BlockSpec((1, HEAD_DIM, N_META), lambda i, g, j: (0, g, 0)),
        ],
        out_specs=pl.BlockSpec((1, group_rows, tq), lambda i, g, j: (i, g, j)),
        out_shape=jax.ShapeDtypeStruct((b, ATTN_WIDTH, n), BF16),
        compiler_params=_params(3),
        name="attention",
    )(qt, k, vt, k_meta, vt_meta)


def _fft_a_kernel(e_ref, x_ref, t_ref, xm_ref, out_ref, meta_ref):
    n1p = out_ref.shape[1]
    width = FOURIER_WIDTH
    residues = e_ref.shape[0]

    @pl.when(pl.program_id(1) == 0)
    def _():
        for q in range(residues):
            meta_ref[q] = _dot(t_ref[q], xm_ref[...])

    for q in range(residues):
        r = _dot(e_ref[q], x_ref[0, :, q * width:(q + 1) * width]) + meta_ref[q]
        for g in range(N_FOURIER_GROUPS):
            base = (q * N_FOURIER_GROUPS + g) * 2 * FOURIER_GROUP
            cols = slice(g * FOURIER_GROUP, (g + 1) * FOURIER_GROUP)
            out_ref[0, :, base:base + FOURIER_GROUP] = r[:n1p, cols].astype(BF16)
            out_ref[0, :, base + FOURIER_GROUP:base + 2 * FOURIER_GROUP] = r[n1p:, cols].astype(BF16)


def _times_unit(re, im, turn):
    quarter = turn * 4.0
    if quarter == round(quarter):
        return [(re, im), (im, -re), (-re, -im), (-im, re)][int(round(quarter)) % 4]
    c, s = math.cos(2.0 * math.pi * turn), -math.sin(2.0 * math.pi * turn)
    return re * c - im * s, re * s + im * c


def _fft_real_part(xs):
    def fft(seq):
        n = len(seq)
        if n == 1:
            return seq
        even, odd = fft(seq[0::2]), fft(seq[1::2])
        out = [None] * n
        for k in range(n // 2):
            tr, ti = _times_unit(*odd[k], k / n)
            out[k] = (even[k][0] + tr, even[k][1] + ti)
            out[k + n // 2] = (even[k][0] - tr, even[k][1] - ti)
        return out

    n = len(xs)
    even, odd = fft(xs[0::2]), fft(xs[1::2])
    out = [None] * n
    for k in range(n // 2):
        tr, _ = _times_unit(*odd[k], k / n)
        out[k] = even[k][0] + tr
        out[k + n // 2] = even[k][0] - tr
    return out


def _fft_b_kernel(ab_ref, mc_ref, out_ref):
    rows = ab_ref.shape[1]
    per_residue = N_FOURIER_GROUPS * 2 * FOURIER_GROUP
    for g in range(N_FOURIER_GROUPS):
        lo = g * 2 * FOURIER_GROUP
        lhs = jnp.concatenate([ab_ref[0, :, q * per_residue + lo:q * per_residue + lo + 2 * FOURIER_GROUP]
                               for q in range(FFT_RADIX)], axis=0)
        c = _dot(lhs, mc_ref[...])
        xs = [(c[q * rows:(q + 1) * rows, :FOURIER_GROUP], c[q * rows:(q + 1) * rows, FOURIER_GROUP:])
              for q in range(FFT_RADIX)]
        for k2, y in enumerate(_fft_real_part(xs)):
            out_ref[0, k2, :, g * FOURIER_GROUP:(g + 1) * FOURIER_GROUP] = y.astype(BF16)


def _fft_row_tile(n1p):
    return max(t for t in range(16, 193, 16) if n1p % t == 0)


def _seq_fft(xv, f_meta, e, t, mc):
    b, m, _ = xv.shape
    width = FOURIER_WIDTH
    n1p = e.shape[1] // 2
    qs = FFT_RADIX if 2 * n1p * m <= FFT_A_SMALL_MATRIX else FFT_A_RESIDUES_PER_STEP
    per_residue = N_FOURIER_GROUPS * 2 * FOURIER_GROUP
    ab = pl.pallas_call(
        _fft_a_kernel,
        grid=(FFT_RADIX // qs, b),
        in_specs=[
            pl.BlockSpec((qs, 2 * n1p, m), lambda q, i: (q, 0, 0)),
            pl.BlockSpec((1, m, qs * width), lambda q, i: (i, 0, q)),
            pl.BlockSpec((qs, 2 * n1p, FFT_RADIX), lambda q, i: (q, 0, 0)),
            pl.BlockSpec((FFT_RADIX, width), lambda q, i: (0, 0)),
        ],
        out_specs=pl.BlockSpec((1, n1p, qs * per_residue), lambda q, i: (i, 0, q)),
        out_shape=jax.ShapeDtypeStruct((b, n1p, FFT_RADIX * per_residue), BF16),
        scratch_shapes=[pltpu.VMEM((qs, 2 * n1p, width), F32)],
        compiler_params=_params(2),
        name="seq_fft_a",
    )(e, xv, t, f_meta)
    tr = _fft_row_tile(n1p)
    return pl.pallas_call(
        _fft_b_kernel,
        grid=(b, n1p // tr),
        in_specs=[
            pl.BlockSpec((1, tr, FFT_RADIX * per_residue), lambda i, r: (i, r, 0)),
            _const_spec(mc.shape),
        ],
        out_specs=pl.BlockSpec((1, FFT_RADIX, tr, width), lambda i, r: (i, 0, r, 0)),
        out_shape=jax.ShapeDtypeStruct((b, FFT_RADIX, n1p, width), BF16),
        compiler_params=_params(2),
        name="seq_fft_b",
    )(ab, mc)


def _seq_fft_tables(n):
    length = n + N_META
    m = n // FFT_RADIX
    n1 = m + 1
    n1p = -(-n1 // 16) * 16
    k1 = jnp.arange(n1p, dtype=jnp.int32)
    n2 = jnp.arange(FFT_RADIX, dtype=jnp.int32)
    valid = (k1 < n1).astype(F32)
    c1, s1 = _unit_circle(k1[:, None] * (FFT_RADIX * (jnp.arange(m, dtype=jnp.int32) + 1))[None, :], length)
    c2, s2 = _unit_circle(n2[:, None] * k1[None, :], length)
    c2, s2 = c2 * valid[None, :], s2 * valid[None, :]
    cos = c1[None, :, :] * c2[:, :, None] - s1[None, :, :] * s2[:, :, None]
    sin = s1[None, :, :] * c2[:, :, None] + c1[None, :, :] * s2[:, :, None]
    e = jnp.concatenate([cos, -sin], axis=1).astype(BF16)
    pick = (n2[:, None, None] == n2[None, None, :]).astype(F32)
    t = jnp.concatenate([c2[:, :, None] * pick, -s2[:, :, None] * pick], axis=1).astype(BF16)
    ch = jnp.arange(FOURIER_GROUP, dtype=jnp.int32)
    cb, sb = _unit_circle(ch[:, None] * ch[None, :], FOURIER_GROUP)
    scale = 1.0 / math.sqrt(length * FOURIER_GROUP)
    mc = jnp.concatenate([jnp.concatenate([cb, -sb], axis=1), jnp.concatenate([sb, cb], axis=1)], axis=0) * scale
    return e, t, mc.astype(BF16)


def _rms(x, g):
    return x * lax.rsqrt(jnp.mean(x * x, axis=-1, keepdims=True) + EPS) * g


def _slab_rows(slabs_ref, tm, n1):
    n1p = slabs_ref.shape[2]
    window = min(FFT_RADIX, -(-(tm - 1) // n1) + 1)
    first_pos = N_META + pl.program_id(1) * tm
    first_slab = jnp.minimum(first_pos // n1, FFT_RADIX - window)
    rel = lax.broadcasted_iota(jnp.int32, (tm, 1), 0) + (first_pos - first_slab * n1)
    target = rel
    for w in range(1, window):
        target = target + jnp.where(rel >= w * n1, n1p - n1, 0)
    col = lax.broadcasted_iota(jnp.int32, (tm, window * n1p), 1)
    pick = (col == target).astype(BF16)
    rows = slabs_ref[0, pl.ds(first_slab, window)].reshape(window * n1p, FOURIER_WIDTH)
    return _dot(pick, rows).astype(BF16)


def _mix_ffn_kernel(x_ref, ot_ref, gate_ref, slabs_ref, wao_ref, wfo_ref, wout_ref, g2_ref, wfi_ref, wfo2_ref,
                    gfin_ref, y_ref, *, ff_chunk, slab_n1):
    x = x_ref[0]
    d_model = x.shape[-1]
    d_ff = wfo2_ref.shape[0]
    attn = ot_ref[0].astype(F32).T.astype(BF16)
    a = _dot(attn, wao_ref[...])
    fo = _dot(_slab_rows(slabs_ref, x.shape[0], slab_n1), wfo_ref[...])
    gates = gate_ref[0]
    merged = gates[:, :d_model].astype(F32) * a + gates[:, d_model:].astype(F32) * fo
    x1 = x + _dot(merged.astype(BF16), wout_ref[...])
    h2 = _rms(x1, g2_ref[...]).astype(BF16)
    acc = x1
    for c in range(d_ff // ff_chunk):
        gt = _dot(h2, wfi_ref[:, c * ff_chunk:(c + 1) * ff_chunk])
        up = _dot(h2, wfi_ref[:, d_ff + c * ff_chunk:d_ff + (c + 1) * ff_chunk])
        act = gt * (1.0 / (1.0 + jnp.exp(-gt))) * up
        acc = acc + _dot(act.astype(BF16), wfo2_ref[c * ff_chunk:(c + 1) * ff_chunk, :])
    y_ref[0] = _rms(acc, gfin_ref[...])


def _mix_ffn(x, ot, four, gates, w_attn_o, w_four_o, w_out, g2, w_ffn_in, w_ffn_out, gfin, tm, ff_chunk):
    b, n, d = x.shape
    d_ff = w_ffn_out.shape[0]
    slab_n1 = n // FFT_RADIX + 1
    four_spec = pl.BlockSpec((1,) + four.shape[1:], lambda i, j: (i, 0, 0, 0), pipeline_mode=pl.Buffered(1))
    return pl.pallas_call(
        functools.partial(_mix_ffn_kernel, ff_chunk=ff_chunk, slab_n1=slab_n1),
        grid=(b, n // tm),
        in_specs=[
            pl.BlockSpec((1, tm, d), lambda i, j: (i, j, 0)),
            pl.BlockSpec((1, ATTN_WIDTH, tm), lambda i, j: (i, 0, j)),
            pl.BlockSpec((1, tm, 2 * d), lambda i, j: (i, j, 0)),
            four_spec,
            _const_spec((ATTN_WIDTH, d)),
            _const_spec((FOURIER_WIDTH, d)),
            _const_spec((d, d)),
            _const_spec((1, d)),
            _const_spec((d, 2 * d_ff)),
            _const_spec((d_ff, d)),
            _const_spec((1, d)),
        ],
        out_specs=pl.BlockSpec((1, tm, d), lambda i, j: (i, j, 0)),
        out_shape=jax.ShapeDtypeStruct((b, n, d), F32),
        compiler_params=_params(2),
        name="mix_ffn",
    )(x, ot, gates, four, w_attn_o, w_four_o, w_out, g2, w_ffn_in, w_ffn_out, gfin)


def _rope_tables(row, col):
    inv_freq = 1.0 / (ROPE_THETA ** (jnp.arange(0, ROPE_AXIS_DIM, 2, dtype=F32) / ROPE_AXIS_DIM))
    ar = row[:, None] * inv_freq[None, :]
    ac = col[:, None] * inv_freq[None, :]
    cos = jnp.concatenate([jnp.cos(ar), jnp.cos(ar), jnp.cos(ac), jnp.cos(ac)], axis=-1)
    sin = jnp.concatenate([-jnp.sin(ar), jnp.sin(ar), -jnp.sin(ac), jnp.sin(ac)], axis=-1)
    reps = LANES // HEAD_DIM
    return jnp.tile(cos, (1, reps)), jnp.tile(sin, (1, reps))


def _unit_circle(prod, length):
    ang = (prod % length).astype(F32) * (2.0 * math.pi / length)
    return jnp.cos(ang), jnp.sin(ang)


def _encode(x, meta_tokens, p):
    b, n, d = x.shape
    tok = jnp.arange(n, dtype=jnp.int32)
    cos, sin = _rope_tables((tok // GRID_W).astype(F32), (tok % GRID_W).astype(F32))
    meta_pad = LANES
    mrow = jnp.full((meta_pad,), -1.0, F32)
    mcol = jnp.arange(meta_pad, dtype=F32)
    cos_m, sin_m = _rope_tables(mrow, mcol)
    x_meta = jnp.zeros((1, meta_pad, d), F32).at[0, :N_META].set(meta_tokens)

    consts = (p["norm_mix_g"], p["w_in"])
    head_consts = (p["gq"], p["gk"], p["kbias"], p["hmean"])
    _, k_m, vt_m, f_m, _ = _in_proj(x_meta, *consts, cos_m, sin_m, *head_consts, tm=meta_pad, residue_major=False)
    qt, k, vt, xv, gates = _in_proj(x, *consts, cos, sin, *head_consts, tm=min(n, IN_PROJ_TILE), residue_major=True)

    attn = functools.partial(_attention, qt, k, vt, k_m[:, :, :N_META], vt_m[:, :, :N_META])
    ot = lax.cond(p["bounded"],
                  functools.partial(attn, tq=min(n, 2 * BOUNDED_Q_SUBTILE), tk=min(n, 256), bounded=True),
                  functools.partial(attn, tq=min(n, 256), tk=min(n, 512), bounded=False))

    four = _seq_fft(xv, f_m[0, :N_META], *_seq_fft_tables(n))

    return _mix_ffn(x, ot, four, gates, p["w_attn_o"], p["w_four_o"], p["w_out"], p["norm_ffn_g"],
                    p["w_ffn_in"], p["w_ffn_out"], p["final_norm_g"], tm=min(n, MIX_FFN_TILE),
                    ff_chunk=p["w_ffn_out"].shape[0])


def kernel(x_prompt, x_sample, meta_tokens, norm_mix_g, w_in, q_norm_g, k_norm_g, w_attn_o, w_four_o, w_out,
           norm_ffn_g, w_ffn_in, w_ffn_out, final_norm_g):
    assert w_in.shape[0] == 1, "single-layer encoder: meta rows are only needed as keys / DFT inputs"
    reps = LANES // HEAD_DIM
    head_id = jnp.arange(2 * LANES) // HEAD_DIM
    bound = 1.01 * LOG2E * HEAD_DIM ** 0.5 * jnp.max(jnp.abs(q_norm_g[0])) * jnp.max(jnp.abs(k_norm_g[0]))
    bounded = bound <= MAX_SCORE_BOUND_LOG2
    p = {
        "bounded": bounded,
        "kbias": jnp.where((jnp.arange(LANES) == HEAD_DIM) & bounded, -bound, 0.0).astype(F32)[None, :],
        "norm_mix_g": norm_mix_g[0][None, :],
        "w_in": w_in[0].astype(BF16),
        "gq": jnp.tile(q_norm_g[0], reps)[None, :],
        "gk": jnp.tile(k_norm_g[0], reps)[None, :],
        "hmean": ((head_id[:, None] == head_id[None, :]).astype(F32) / HEAD_DIM).astype(BF16),
        "w_attn_o": w_attn_o[0].astype(BF16),
        "w_four_o": w_four_o[0].astype(BF16),
        "w_out": w_out[0].astype(BF16),
        "norm_ffn_g": norm_ffn_g[0][None, :],
        "w_ffn_in": w_ffn_in[0].astype(BF16),
        "w_ffn_out": w_ffn_out[0].astype(BF16),
        "final_norm_g": final_norm_g[None, :],
    }
    return (_encode(x_prompt, meta_tokens, p), _encode(x_sample, meta_tokens, p))
```

```python
import functools
import math

import jax
import jax.numpy as jnp
from jax import lax
from jax.experimental import pallas as pl
from jax.experimental.pallas import tpu as pltpu

N_META = 16
GRID_W = 64
HEAD_DIM = 64
N_Q_HEADS = 8
N_KV_HEADS = 2
Q_PER_KV = N_Q_HEADS // N_KV_HEADS
ATTN_WIDTH = N_Q_HEADS * HEAD_DIM
KV_WIDTH = N_KV_HEADS * HEAD_DIM
FOURIER_WIDTH = 512
FOURIER_GROUP = 128
N_FOURIER_GROUPS = FOURIER_WIDTH // FOURIER_GROUP
ROPE_AXIS_DIM = HEAD_DIM // 2
ROPE_THETA = 10000.0
EPS = 1e-6

LANES = 128
VMEM_LIMIT_BYTES = 56 * 1024 * 1024

F32 = jnp.float32
BF16 = jnp.bfloat16

LOG2E = math.log2(math.e)
Q_SCALE = HEAD_DIM ** -0.5 * LOG2E
MAX_SCORE_BOUND_LOG2 = 60.0
BOUNDED_Q_SUBTILE = 2 * LANES

FFT_RADIX = N_META
FFT_A_RESIDUES_PER_STEP = 4
FFT_A_SMALL_MATRIX = 1 << 16

IN_PROJ_TILE = 512
IN_PROJ_SUBTILE = 512
MIX_FFN_TILE = 512
MIX_FFN_SUBTILE = 256


def _dot(a, b):
    return jnp.dot(a, b, preferred_element_type=F32)


def _const_spec(shape):
    zeros = (0,) * len(shape)
    return pl.BlockSpec(shape, lambda *_: zeros, pipeline_mode=pl.Buffered(1))


def _params(n_axes):
    return pltpu.CompilerParams(dimension_semantics=("arbitrary",) * n_axes,
                                vmem_limit_bytes=VMEM_LIMIT_BYTES)


def _in_proj_kernel(x_ref, g_ref, w_ref, cos_ref, sin_ref, gq_ref, gk_ref, kbias_ref, hmean_ref,
                    qt_ref, k_ref, vt_ref, z_ref, gate_ref, *slab_ref):
    tm, d_model = x_ref.shape[1:]
    sub = min(tm, IN_PROJ_SUBTILE)
    o0 = ATTN_WIDTH
    o1 = o0 + KV_WIDTH
    o2 = o1 + KV_WIDTH
    o3 = o2 + FOURIER_WIDTH
    tile = 2 * LANES
    lane = lax.broadcasted_iota(jnp.int32, (sub, LANES), 1)
    first_half = (lane % ROPE_AXIS_DIM) < (ROPE_AXIS_DIM // 2)
    low = lane < HEAD_DIM

    def head_norm(z2):
        return z2 * lax.rsqrt(_dot((z2 * z2).astype(BF16), hmean_ref[...]) + EPS)

    spans = [slice(r, r + sub) for r in range(0, tm, sub)]
    hs = []
    for rows in spans:
        x = x_ref[0, rows, :]
        hs.append((x * lax.rsqrt(jnp.mean(x * x, axis=-1, keepdims=True) + EPS) * g_ref[...]).astype(BF16))

    for s, (rows, h) in enumerate(zip(spans, hs)):
        cos = cos_ref[rows, :]
        sin = sin_ref[rows, :]

        def rope(y):
            half = ROPE_AXIS_DIM // 2
            partner = jnp.where(first_half, pltpu.roll(y, LANES - half, 1), pltpu.roll(y, half, 1))
            return y * cos + partner * sin

        gate_chunk = 512
        n_gate_chunks = 2 * d_model // gate_chunk

        def gates(c):
            zg = _dot(h, w_ref[:, o3 + c * gate_chunk:o3 + (c + 1) * gate_chunk])
            gate_ref[0, rows, c * gate_chunk:(c + 1) * gate_chunk] = (1.0 / (1.0 + jnp.exp(-zg))).astype(BF16)

        zq = _dot(h, w_ref[:, :o0])
        for c in range(ATTN_WIDTH // tile):
            yn = head_norm(zq[:, c * tile:(c + 1) * tile])
            for cc in range(tile // LANES):
                yq = rope(yn[:, cc * LANES:(cc + 1) * LANES] * gq_ref[...]) * Q_SCALE
                r0 = c * tile + cc * LANES
                qt_ref[0, r0:r0 + LANES, rows] = yq.T.astype(BF16)

        zkv = _dot(h, w_ref[:, o0:o2])
        yk = rope(head_norm(zkv)[:, :KV_WIDTH] * gk_ref[...])
        kbias = kbias_ref[...]
        k_ref[0, 0, rows, :] = jnp.where(low, yk, kbias).astype(BF16)
        k_ref[0, 1, rows, :] = jnp.where(low, pltpu.roll(yk, HEAD_DIM, 1), kbias).astype(BF16)

        vt_ref[0, :, rows] = zkv[:, KV_WIDTH:].T.astype(BF16)

        zf = _dot(h, w_ref[:, o2:o3])
        if not slab_ref:
            z_ref[0, rows, :] = zf.astype(BF16)
        else:
            slabs, = slab_ref
            n_rows = sub // FFT_RADIX
            for g in range(N_FOURIER_GROUPS):
                slabs[s, g] = zf[:, g * FOURIER_GROUP:(g + 1) * FOURIER_GROUP]
            for j in range(FFT_RADIX):
                for g in range(N_FOURIER_GROUPS):
                    lo = j * FOURIER_WIDTH + g * FOURIER_GROUP
                    z_ref[0, s * n_rows:(s + 1) * n_rows, lo:lo + FOURIER_GROUP] = (
                        slabs[s, g, pl.ds(j, n_rows, stride=FFT_RADIX), :].astype(BF16))

        for c in range(n_gate_chunks):
            gates(c)


def _in_proj(x, norm_g, w_in, cos, sin, gq, gk, kbias, hmean, tm, residue_major):
    b, n, d = x.shape
    in_width = w_in.shape[1]
    grid = (b, n // tm)
    if residue_major:
        four_rows, four_tile, four_width = n // FFT_RADIX, tm // FFT_RADIX, FFT_RADIX * FOURIER_WIDTH
        sub = min(tm, IN_PROJ_SUBTILE)
        scratch = [pltpu.VMEM((tm // sub, N_FOURIER_GROUPS, sub, FOURIER_GROUP), F32)]
    else:
        four_rows, four_tile, four_width = n, tm, FOURIER_WIDTH
        scratch = []
    out_shape = (
        jax.ShapeDtypeStruct((b, ATTN_WIDTH, n), BF16),
        jax.ShapeDtypeStruct((b, N_KV_HEADS, n, LANES), BF16),
        jax.ShapeDtypeStruct((b, KV_WIDTH, n), BF16),
        jax.ShapeDtypeStruct((b, four_rows, four_width), BF16),
        jax.ShapeDtypeStruct((b, n, 2 * d), BF16),
    )
    return pl.pallas_call(
        _in_proj_kernel,
        grid=grid,
        in_specs=[
            pl.BlockSpec((1, tm, d), lambda i, j: (i, j, 0)),
            _const_spec((1, d)),
            _const_spec((d, in_width)),
            pl.BlockSpec((tm, LANES), lambda i, j: (j, 0)),
            pl.BlockSpec((tm, LANES), lambda i, j: (j, 0)),
            _const_spec((1, LANES)),
            _const_spec((1, LANES)),
            _const_spec((1, LANES)),
            _const_spec((2 * LANES, 2 * LANES)),
        ],
        out_specs=(
            pl.BlockSpec((1, ATTN_WIDTH, tm), lambda i, j: (i, 0, j)),
            pl.BlockSpec((1, N_KV_HEADS, tm, LANES), lambda i, j: (i, 0, j, 0)),
            pl.BlockSpec((1, KV_WIDTH, tm), lambda i, j: (i, 0, j)),
            pl.BlockSpec((1, four_tile, four_width), lambda i, j: (i, j, 0)),
            pl.BlockSpec((1, tm, 2 * d), lambda i, j: (i, j, 0)),
        ),
        out_shape=out_shape,
        scratch_shapes=scratch,
        compiler_params=_params(2),
        name="in_proj",
    )(x, norm_g, w_in, cos, sin, gq, gk, kbias, hmean)


def _attn_kernel(qt_ref, k_ref, vt_ref, km_ref, vtm_ref, ot_ref, *, tk, bounded):
    tq = qt_ref.shape[2]
    n_keys = k_ref.shape[2]
    n_blocks = n_keys // tk
    heads = range(Q_PER_KV)

    def head_qt(h, lo, width):
        ones_row = (lax.broadcasted_iota(jnp.int32, (HEAD_DIM, width), 0) == 0).astype(BF16)
        return jnp.concatenate([qt_ref[0, h * HEAD_DIM:(h + 1) * HEAD_DIM, lo:lo + width], ones_row], axis=0)

    if bounded:
        sub = min(tq, BOUNDED_Q_SUBTILE)
        for lo in range(0, tq, sub):
            qts = [head_qt(h, lo, sub) for h in heads]
            p = [jnp.exp2(_dot(km_ref[0, 0], qts[h])) for h in heads]
            l = [jnp.sum(p[h], axis=0, keepdims=True) for h in heads]
            acc = [_dot(vtm_ref[0], p[h].astype(BF16)) for h in heads]
            s = [_dot(k_ref[0, 0, 0:tk, :], qts[h]) for h in heads]
            for j in range(n_blocks):
                for h in heads:
                    ph = jnp.exp2(s[h])
                    l[h] = l[h] + jnp.sum(ph, axis=0, keepdims=True)
                    if j + 1 < n_blocks:
                        s[h] = _dot(k_ref[0, 0, (j + 1) * tk:(j + 2) * tk, :], qts[h])
                    acc[h] = acc[h] + _dot(vt_ref[0, :, j * tk:(j + 1) * tk], ph.astype(BF16))
            for h in heads:
                ot_ref[0, h * HEAD_DIM:(h + 1) * HEAD_DIM, lo:lo + sub] = (acc[h] / l[h]).astype(BF16)
        return

    qt = jnp.concatenate([head_qt(h, 0, tq) for h in heads], axis=1)

    def block(kb, vb, m, l, acc):
        s = _dot(kb, qt)
        m_new = jnp.maximum(m, jnp.max(s, axis=0, keepdims=True))
        alpha = jnp.exp2(m - m_new)
        p = jnp.exp2(s - m_new)
        return (m_new, alpha * l + jnp.sum(p, axis=0, keepdims=True), alpha * acc + _dot(vb, p.astype(BF16)))

    width = qt.shape[1]
    init = (jnp.full((1, width), -jnp.inf, F32), jnp.zeros((1, width), F32), jnp.zeros((HEAD_DIM, width), F32))
    carry = block(km_ref[0, 0], vtm_ref[0], *init)

    def body(j, carry):
        start = pl.multiple_of(j * tk, tk)
        return block(k_ref[0, 0, pl.ds(start, tk), :], vt_ref[0, :, pl.ds(start, tk)], *carry)

    _, l, acc = lax.fori_loop(0, n_blocks, body, carry)
    out = (acc / l).astype(BF16)
    for h in range(Q_PER_KV):
        ot_ref[0, h * HEAD_DIM:(h + 1) * HEAD_DIM, :] = out[:, h * tq:(h + 1) * tq]


def _attention(qt, k, vt, k_meta, vt_meta, tq, tk, bounded):
    b, _, n = qt.shape
    group_rows = Q_PER_KV * HEAD_DIM
    return pl.pallas_call(
        functools.partial(_attn_kernel, tk=tk, bounded=bounded),
        grid=(b, N_KV_HEADS, n // tq),
        in_specs=[
            pl.BlockSpec((1, group_rows, tq), lambda i, g, j: (i, g, j)),
            pl.BlockSpec((1, 1, n, LANES), lambda i, g, j: (i, g, 0, 0)),
            pl.BlockSpec((1, HEAD_DIM, n), lambda i, g, j: (i, g, 0)),
            pl.BlockSpec((1, 1, N_META, LANES), lambda i, g, j: (0, g, 0, 0)),
            pl.BlockSpec((1, HEAD_DIM, N_META), lambda i, g, j: (0, g, 0)),
        ],
        out_specs=pl.BlockSpec((1, group_rows, tq), lambda i, g, j: (i, g, j)),
        out_shape=jax.ShapeDtypeStruct((b, ATTN_WIDTH, n), BF16),
        compiler_params=_params(3),
        name="attention",
    )(qt, k, vt, k_meta, vt_meta)


def _fft_a_kernel(e_ref, x_ref, t_ref, xm_ref, out_ref, meta_ref):
    n1p = out_ref.shape[1]
    width = FOURIER_WIDTH
    residues = e_ref.shape[0]

    @pl.when(pl.program_id(1) == 0)
    def _():
        for q in range(residues):
            meta_ref[q] = _dot(t_ref[q], xm_ref[...])

    for q in range(residues):
        r = _dot(e_ref[q], x_ref[0, :, q * width:(q + 1) * width]) + meta_ref[q]
        for g in range(N_FOURIER_GROUPS):
            base = (q * N_FOURIER_GROUPS + g) * 2 * FOURIER_GROUP
            cols = slice(g * FOURIER_GROUP, (g + 1) * FOURIER_GROUP)
            out_ref[0, :, base:base + FOURIER_GROUP] = r[:n1p, cols].astype(BF16)
            out_ref[0, :, base + FOURIER_GROUP:base + 2 * FOURIER_GROUP] = r[n1p:, cols].astype(BF16)


def _times_unit(re, im, turn):
    quarter = turn * 4.0
    if quarter == round(quarter):
        return [(re, im), (im, -re), (-re, -im), (-im, re)][int(round(quarter)) % 4]
    c, s = math.cos(2.0 * math.pi * turn), -math.sin(2.0 * math.pi * turn)
    return re * c - im * s, re * s + im * c


def _fft_real_part(xs):
    def fft(seq):
        n = len(seq)
        if n == 1:
            return seq
        even, odd = fft(seq[0::2]), fft(seq[1::2])
        out = [None] * n
        for k in range(n // 2):
            tr, ti = _times_unit(*odd[k], k / n)
            out[k] = (even[k][0] + tr, even[k][1] + ti)
            out[k + n // 2] = (even[k][0] - tr, even[k][1] - ti)
        return out

    n = len(xs)
    even, odd = fft(xs[0::2]), fft(xs[1::2])
    out = [None] * n
    for k in range(n // 2):
        tr, _ = _times_unit(*odd[k], k / n)
        out[k] = even[k][0] + tr
        out[k + n // 2] = even[k][0] - tr
    return out


def _fft_b_kernel(ab_ref, mc_ref, out_ref):
    rows = ab_ref.shape[1]
    per_residue = N_FOURIER_GROUPS * 2 * FOURIER_GROUP
    for g in range(N_FOURIER_GROUPS):
        lo = g * 2 * FOURIER_GROUP
        lhs = jnp.concatenate([ab_ref[0, :, q * per_residue + lo:q * per_residue + lo + 2 * FOURIER_GROUP]
                               for q in range(FFT_RADIX)], axis=0)
        c = _dot(lhs, mc_ref[...])
        xs = [(c[q * rows:(q + 1) * rows, :FOURIER_GROUP], c[q * rows:(q + 1) * rows, FOURIER_GROUP:])
              for q in range(FFT_RADIX)]
        for k2, y in enumerate(_fft_real_part(xs)):
            out_ref[0, k2, :, g * FOURIER_GROUP:(g + 1) * FOURIER_GROUP] = y.astype(BF16)


def _fft_row_tile(n1p):
    return max(t for t in range(16, 193, 16) if n1p % t == 0)


def _seq_fft(xv, f_meta, e, t, mc):
    b, m, _ = xv.shape
    width = FOURIER_WIDTH
    n1p = e.shape[1] // 2
    qs = FFT_RADIX if 2 * n1p * m <= FFT_A_SMALL_MATRIX else FFT_A_RESIDUES_PER_STEP
    per_residue = N_FOURIER_GROUPS * 2 * FOURIER_GROUP
    ab = pl.pallas_call(
        _fft_a_kernel,
        grid=(FFT_RADIX // qs, b),
        in_specs=[
            pl.BlockSpec((qs, 2 * n1p, m), lambda q, i: (q, 0, 0)),
            pl.BlockSpec((1, m, qs * width), lambda q, i: (i, 0, q)),
            pl.BlockSpec((qs, 2 * n1p, FFT_RADIX), lambda q, i: (q, 0, 0)),
            pl.BlockSpec((FFT_RADIX, width), lambda q, i: (0, 0)),
        ],
        out_specs=pl.BlockSpec((1, n1p, qs * per_residue), lambda q, i: (i, 0, q)),
        out_shape=jax.ShapeDtypeStruct((b, n1p, FFT_RADIX * per_residue), BF16),
        scratch_shapes=[pltpu.VMEM((qs, 2 * n1p, width), F32)],
        compiler_params=_params(2),
        name="seq_fft_a",
    )(e, xv, t, f_meta)
    tr = _fft_row_tile(n1p)
    return pl.pallas_call(
        _fft_b_kernel,
        grid=(b, n1p // tr),
        in_specs=[
            pl.BlockSpec((1, tr, FFT_RADIX * per_residue), lambda i, r: (i, r, 0)),
            _const_spec(mc.shape),
        ],
        out_specs=pl.BlockSpec((1, FFT_RADIX, tr, width), lambda i, r: (i, 0, r, 0)),
        out_shape=jax.ShapeDtypeStruct((b, FFT_RADIX, n1p, width), BF16),
        compiler_params=_params(2),
        name="seq_fft_b",
    )(ab, mc)


def _seq_fft_tables(n):
    length = n + N_META
    m = n // FFT_RADIX
    n1 = m + 1
    n1p = -(-n1 // 16) * 16
    k1 = jnp.arange(n1p, dtype=jnp.int32)
    n2 = jnp.arange(FFT_RADIX, dtype=jnp.int32)
    valid = (k1 < n1).astype(F32)
    c1, s1 = _unit_circle(k1[:, None] * (FFT_RADIX * (jnp.arange(m, dtype=jnp.int32) + 1))[None, :], length)
    c2, s2 = _unit_circle(n2[:, None] * k1[None, :], length)
    c2, s2 = c2 * valid[None, :], s2 * valid[None, :]
    cos = c1[None, :, :] * c2[:, :, None] - s1[None, :, :] * s2[:, :, None]
    sin = s1[None, :, :] * c2[:, :, None] + c1[None, :, :] * s2[:, :, None]
    e = jnp.concatenate([cos, -sin], axis=1).astype(BF16)
    pick = (n2[:, None, None] == n2[None, None, :]).astype(F32)
    t = jnp.concatenate([c2[:, :, None] * pick, -s2[:, :, None] * pick], axis=1).astype(BF16)
    ch = jnp.arange(FOURIER_GROUP, dtype=jnp.int32)
    cb, sb = _unit_circle(ch[:, None] * ch[None, :], FOURIER_GROUP)
    scale = 1.0 / math.sqrt(length * FOURIER_GROUP)
    mc = jnp.concatenate([jnp.concatenate([cb, -sb], axis=1), jnp.concatenate([sb, cb], axis=1)], axis=0) * scale
    return e, t, mc.astype(BF16)


def _rms(x, g):
    return x * lax.rsqrt(jnp.mean(x * x, axis=-1, keepdims=True) + EPS) * g


def _slab_rows(slabs_ref, first_pos, tm, n1):
    n1p = slabs_ref.shape[2]
    window = min(FFT_RADIX, -(-(tm - 1) // n1) + 1)
    span = min(n1p, -(-(tm + 15) // 16) * 16)
    first_slab = first_pos // n1
    rel0 = first_pos - first_slab * n1
    row0 = pl.multiple_of(jnp.minimum((rel0 // 16) * 16, n1p - span), 16)
    rel = lax.broadcasted_iota(jnp.int32, (tm, 1), 0) + rel0
    target = rel - row0
    for w in range(1, window):
        target = target + jnp.where(rel >= w * n1, span - n1 + (row0 if w == 1 else 0), 0)
    col = lax.broadcasted_iota(jnp.int32, (tm, window * span), 1)
    pick = (col == target).astype(BF16)
    rows = jnp.concatenate([slabs_ref[0, first_slab, pl.ds(row0, span), :]]
                           + [slabs_ref[0, jnp.minimum(first_slab + w, FFT_RADIX - 1), 0:span, :]
                              for w in range(1, window)], axis=0)
    return _dot(pick, rows).astype(BF16)


def _mix_ffn_kernel(x_ref, ot_ref, gate_ref, slabs_ref, wao_ref, wfo_ref, wout_ref, g2_ref, wfi_ref, wfo2_ref,
                    gfin_ref, y_ref, *, ff_chunk, slab_n1):
    tm, d_model = x_ref.shape[1:]
    d_ff = wfo2_ref.shape[0]
    sub = min(tm, MIX_FFN_SUBTILE)
    spans = [slice(r, r + sub) for r in range(0, tm, sub)]
    a, fo, x1, h2 = [], [], [], []
    for rows in spans:
        attn = ot_ref[0, :, rows].astype(F32).T.astype(BF16)
        a.append(_dot(attn, wao_ref[...]))
        four = _slab_rows(slabs_ref, N_META + pl.program_id(1) * tm + rows.start, sub, slab_n1)
        fo.append(_dot(four, wfo_ref[...]))
    for s, rows in enumerate(spans):
        gates = gate_ref[0, rows, :]
        merged = gates[:, :d_model].astype(F32) * a[s] + gates[:, d_model:].astype(F32) * fo[s]
        x1.append(x_ref[0, rows, :] + _dot(merged.astype(BF16), wout_ref[...]))
    for s in range(len(spans)):
        h2.append(_rms(x1[s], g2_ref[...]).astype(BF16))
    for s, rows in enumerate(spans):
        acc = x1[s]
        for c in range(d_ff // ff_chunk):
            gt = _dot(h2[s], wfi_ref[:, c * ff_chunk:(c + 1) * ff_chunk])
            up = _dot(h2[s], wfi_ref[:, d_ff + c * ff_chunk:d_ff + (c + 1) * ff_chunk])
            act = gt * (1.0 / (1.0 + jnp.exp(-gt))) * up
            acc = acc + _dot(act.astype(BF16), wfo2_ref[c * ff_chunk:(c + 1) * ff_chunk, :])
        y_ref[0, rows, :] = _rms(acc, gfin_ref[...])


def _mix_ffn(x, ot, four, gates, w_attn_o, w_four_o, w_out, g2, w_ffn_in, w_ffn_out, gfin, tm, ff_chunk):
    b, n, d = x.shape
    d_ff = w_ffn_out.shape[0]
    slab_n1 = n // FFT_RADIX + 1
    four_spec = pl.BlockSpec((1,) + four.shape[1:], lambda i, j: (i, 0, 0, 0), pipeline_mode=pl.Buffered(1))
    return pl.pallas_call(
        functools.partial(_mix_ffn_kernel, ff_chunk=ff_chunk, slab_n1=slab_n1),
        grid=(b, n // tm),
        in_specs=[
            pl.BlockSpec((1, tm, d), lambda i, j: (i, j, 0)),
            pl.BlockSpec((1, ATTN_WIDTH, tm), lambda i, j: (i, 0, j)),
            pl.BlockSpec((1, tm, 2 * d), lambda i, j: (i, j, 0)),
            four_spec,
            _const_spec((ATTN_WIDTH, d)),
            _const_spec((FOURIER_WIDTH, d)),
            _const_spec((d, d)),
            _const_spec((1, d)),
            _const_spec((d, 2 * d_ff)),
            _const_spec((d_ff, d)),
            _const_spec((1, d)),
        ],
        out_specs=pl.BlockSpec((1, tm, d), lambda i, j: (i, j, 0)),
        out_shape=jax.ShapeDtypeStruct((b, n, d), F32),
        compiler_params=_params(2),
        name="mix_ffn",
    )(x, ot, gates, four, w_attn_o, w_four_o, w_out, g2, w_ffn_in, w_ffn_out, gfin)


def _rope_tables(row, col):
    inv_freq = 1.0 / (ROPE_THETA ** (jnp.arange(0, ROPE_AXIS_DIM, 2, dtype=F32) / ROPE_AXIS_DIM))
    ar = row[:, None] * inv_freq[None, :]
    ac = col[:, None] * inv_freq[None, :]
    cos = jnp.concatenate([jnp.cos(ar), jnp.cos(ar), jnp.cos(ac), jnp.cos(ac)], axis=-1)
    sin = jnp.concatenate([-jnp.sin(ar), jnp.sin(ar), -jnp.sin(ac), jnp.sin(ac)], axis=-1)
    reps = LANES // HEAD_DIM
    return jnp.tile(cos, (1, reps)), jnp.tile(sin, (1, reps))


def _unit_circle(prod, length):
    ang = (prod % length).astype(F32) * (2.0 * math.pi / length)
    return jnp.cos(ang), jnp.sin(ang)


def _encode(x, meta_tokens, p):
    b, n, d = x.shape
    tok = jnp.arange(n, dtype=jnp.int32)
    cos, sin = _rope_tables((tok // GRID_W).astype(F32), (tok % GRID_W).astype(F32))
    meta_pad = LANES
    mrow = jnp.full((meta_pad,), -1.0, F32)
    mcol = jnp.arange(meta_pad, dtype=F32)
    cos_m, sin_m = _rope_tables(mrow, mcol)
    x_meta = jnp.zeros((1, meta_pad, d), F32).at[0, :N_META].set(meta_tokens)

    consts = (p["norm_mix_g"], p["w_in"])
    head_consts = (p["gq"], p["gk"], p["kbias"], p["hmean"])
    _, k_m, vt_m, f_m, _ = _in_proj(x_meta, *consts, cos_m, sin_m, *head_consts, tm=meta_pad, residue_major=False)
    qt, k, vt, xv, gates = _in_proj(x, *consts, cos, sin, *head_consts, tm=min(n, IN_PROJ_TILE), residue_major=True)

    attn = functools.partial(_attention, qt, k, vt, k_m[:, :, :N_META], vt_m[:, :, :N_META])
    ot = lax.cond(p["bounded"],
                  functools.partial(attn, tq=min(n, 2 * BOUNDED_Q_SUBTILE), tk=min(n, 256), bounded=True),
                  functools.partial(attn, tq=min(n, 256), tk=min(n, 512), bounded=False))

    four = _seq_fft(xv, f_m[0, :N_META], *_seq_fft_tables(n))

    return _mix_ffn(x, ot, four, gates, p["w_attn_o"], p["w_four_o"], p["w_out"], p["norm_ffn_g"],
                    p["w_ffn_in"], p["w_ffn_out"], p["final_norm_g"], tm=min(n, MIX_FFN_TILE),
                    ff_chunk=p["w_ffn_out"].shape[0])


def kernel(x_prompt, x_sample, meta_tokens, norm_mix_g, w_in, q_norm_g, k_norm_g, w_attn_o, w_four_o, w_out,
           norm_ffn_g, w_ffn_in, w_ffn_out, final_norm_g):
    assert w_in.shape[0] == 1, "single-layer encoder: meta rows are only needed as keys / DFT inputs"
    reps = LANES // HEAD_DIM
    head_id = jnp.arange(2 * LANES) // HEAD_DIM
    bound = 1.01 * LOG2E * HEAD_DIM ** 0.5 * jnp.max(jnp.abs(q_norm_g[0])) * jnp.max(jnp.abs(k_norm_g[0]))
    bounded = bound <= MAX_SCORE_BOUND_LOG2
    p = {
        "bounded": bounded,
        "kbias": jnp.where((jnp.arange(LANES) == HEAD_DIM) & bounded, -bound, 0.0).astype(F32)[None, :],
        "norm_mix_g": norm_mix_g[0][None, :],
        "w_in": w_in[0].astype(BF16),
        "gq": jnp.tile(q_norm_g[0], reps)[None, :],
        "gk": jnp.tile(k_norm_g[0], reps)[None, :],
        "hmean": ((head_id[:, None] == head_id[None, :]).astype(F32) / HEAD_DIM).astype(BF16),
        "w_attn_o": w_attn_o[0].astype(BF16),
        "w_four_o": w_four_o[0].astype(BF16),
        "w_out": w_out[0].astype(BF16),
        "norm_ffn_g": norm_ffn_g[0][None, :],
        "w_ffn_in": w_ffn_in[0].astype(BF16),
        "w_ffn_out": w_ffn_out[0].astype(BF16),
        "final_norm_g": final_norm_g[None, :],
    }
    return (_encode(x_prompt, meta_tokens, p), _encode(x_sample, meta_tokens, p))
```

```python
import functools
import math

import jax
import jax.numpy as jnp
from jax import lax
from jax.experimental import pallas as pl
from jax.experimental.pallas import tpu as pltpu

N_META = 16
GRID_W = 64
HEAD_DIM = 64
N_Q_HEADS = 8
N_KV_HEADS = 2
Q_PER_KV = N_Q_HEADS // N_KV_HEADS
ATTN_WIDTH = N_Q_HEADS * HEAD_DIM
KV_WIDTH = N_KV_HEADS * HEAD_DIM
FOURIER_WIDTH = 512
FOURIER_GROUP = 128
N_FOURIER_GROUPS = FOURIER_WIDTH // FOURIER_GROUP
ROPE_AXIS_DIM = HEAD_DIM // 2
ROPE_THETA = 10000.0
EPS = 1e-6

LANES = 128
VMEM_LIMIT_BYTES = 56 * 1024 * 1024

F32 = jnp.float32
BF16 = jnp.bfloat16

LOG2E = math.log2(math.e)
Q_SCALE = HEAD_DIM ** -0.5 * LOG2E
MAX_SCORE_BOUND_LOG2 = 60.0
BOUNDED_Q_SUBTILE = 2 * LANES

FFT_RADIX = N_META
FFT_A_RESIDUES_PER_STEP = 4
FFT_A_SMALL_MATRIX = 1 << 16

IN_PROJ_TILE = 512
IN_PROJ_SUBTILE = 512
MIX_FFN_TILE = 512
MIX_FFN_SUBTILE = 256


def _dot(a, b):
    return jnp.dot(a, b, preferred_element_type=F32)


def _const_spec(shape):
    zeros = (0,) * len(shape)
    return pl.BlockSpec(shape, lambda *_: zeros, pipeline_mode=pl.Buffered(1))


def _params(n_axes):
    return pltpu.CompilerParams(dimension_semantics=("arbitrary",) * n_axes,
                                vmem_limit_bytes=VMEM_LIMIT_BYTES)


def _in_proj_kernel(x_ref, g_ref, w_ref, cos_ref, sin_ref, qa_ref, qb_ref, gk_ref, kbias_ref, hmean_ref,
                    qt_ref, k_ref, vt_ref, z_ref, gate_ref, *slab_ref):
    tm, d_model = x_ref.shape[1:]
    sub = min(tm, IN_PROJ_SUBTILE)
    o0 = ATTN_WIDTH
    o1 = o0 + KV_WIDTH
    o2 = o1 + KV_WIDTH
    o3 = o2 + FOURIER_WIDTH
    tile = 2 * LANES
    lane = lax.broadcasted_iota(jnp.int32, (sub, LANES), 1)
    first_half = (lane % ROPE_AXIS_DIM) < (ROPE_AXIS_DIM // 2)
    low = lane < HEAD_DIM

    def head_norm(z2):
        return z2 * lax.rsqrt(_dot((z2 * z2).astype(BF16), hmean_ref[...]) + EPS)

    spans = [slice(r, r + sub) for r in range(0, tm, sub)]
    subs = range(len(spans))
    hs = []
    for rows in spans:
        x = x_ref[0, rows, :]
        hs.append((x * lax.rsqrt(jnp.mean(x * x, axis=-1, keepdims=True) + EPS) * g_ref[...]).astype(BF16))

    def rope(y, rows):
        half = ROPE_AXIS_DIM // 2
        partner = jnp.where(first_half, pltpu.roll(y, LANES - half, 1), pltpu.roll(y, half, 1))
        return y * cos_ref[rows, :] + partner * sin_ref[rows, :]

    zq = [_dot(hs[s], w_ref[:, :o0]) for s in subs]
    quarter = ROPE_AXIS_DIM // 2
    for s, rows in enumerate(spans):
        for c in range(ATTN_WIDTH // LANES):
            zt = zq[s][:, c * LANES:(c + 1) * LANES].T
            for hh in range(LANES // HEAD_DIM):
                blk = zt[hh * HEAD_DIM:(hh + 1) * HEAD_DIM, :]
                y = blk * lax.rsqrt(jnp.mean(blk * blk, axis=0, keepdims=True) + EPS)
                partner = jnp.concatenate([y[quarter:2 * quarter], y[:quarter], y[3 * quarter:], y[2 * quarter:3 * quarter]],
                                          axis=0)
                r0 = c * LANES + hh * HEAD_DIM
                qt_ref[0, r0:r0 + HEAD_DIM, rows] = (y * qa_ref[:, rows] + partner * qb_ref[:, rows]).astype(BF16)

    zkv = [_dot(hs[s], w_ref[:, o0:o2]) for s in subs]
    kbias = kbias_ref[...]
    for s, rows in enumerate(spans):
        yk = rope(head_norm(zkv[s])[:, :KV_WIDTH] * gk_ref[...], rows)
        k_ref[0, 0, rows, :] = jnp.where(low, yk, kbias).astype(BF16)
        k_ref[0, 1, rows, :] = jnp.where(low, pltpu.roll(yk, HEAD_DIM, 1), kbias).astype(BF16)
        vt_ref[0, :, rows] = zkv[s][:, KV_WIDTH:].T.astype(BF16)

    for s, rows in enumerate(spans):
        zf = _dot(hs[s], w_ref[:, o2:o3])
        if not slab_ref:
            z_ref[0, rows, :] = zf.astype(BF16)
        else:
            slabs, = slab_ref
            n_rows = sub // FFT_RADIX
            for g in range(N_FOURIER_GROUPS):
                slabs[s, g] = zf[:, g * FOURIER_GROUP:(g + 1) * FOURIER_GROUP]
            for j in range(FFT_RADIX):
                for g in range(N_FOURIER_GROUPS):
                    lo = j * FOURIER_WIDTH + g * FOURIER_GROUP
                    z_ref[0, s * n_rows:(s + 1) * n_rows, lo:lo + FOURIER_GROUP] = (
                        slabs[s, g, pl.ds(j, n_rows, stride=FFT_RADIX), :].astype(BF16))

    gate_chunk = 512
    for c in range(2 * d_model // gate_chunk):
        for s, rows in enumerate(spans):
            zg = _dot(hs[s], w_ref[:, o3 + c * gate_chunk:o3 + (c + 1) * gate_chunk])
            gate_ref[0, rows, c * gate_chunk:(c + 1) * gate_chunk] = (1.0 / (1.0 + jnp.exp(-zg))).astype(BF16)


def _in_proj(x, norm_g, w_in, cos, sin, qa, qb, gk, kbias, hmean, tm, residue_major):
    b, n, d = x.shape
    in_width = w_in.shape[1]
    grid = (b, n // tm)
    if residue_major:
        four_rows, four_tile, four_width = n // FFT_RADIX, tm // FFT_RADIX, FFT_RADIX * FOURIER_WIDTH
        sub = min(tm, IN_PROJ_SUBTILE)
        scratch = [pltpu.VMEM((tm // sub, N_FOURIER_GROUPS, sub, FOURIER_GROUP), F32)]
    else:
        four_rows, four_tile, four_width = n, tm, FOURIER_WIDTH
        scratch = []
    out_shape = (
        jax.ShapeDtypeStruct((b, ATTN_WIDTH, n), BF16),
        jax.ShapeDtypeStruct((b, N_KV_HEADS, n, LANES), BF16),
        jax.ShapeDtypeStruct((b, KV_WIDTH, n), BF16),
        jax.ShapeDtypeStruct((b, four_rows, four_width), BF16),
        jax.ShapeDtypeStruct((b, n, 2 * d), BF16),
    )
    return pl.pallas_call(
        _in_proj_kernel,
        grid=grid,
        in_specs=[
            pl.BlockSpec((1, tm, d), lambda i, j: (i, j, 0)),
            _const_spec((1, d)),
            _const_spec((d, in_width)),
            pl.BlockSpec((tm, LANES), lambda i, j: (j, 0)),
            pl.BlockSpec((tm, LANES), lambda i, j: (j, 0)),
            pl.BlockSpec((HEAD_DIM, tm), lambda i, j: (0, j)),
            pl.BlockSpec((HEAD_DIM, tm), lambda i, j: (0, j)),
            _const_spec((1, LANES)),
            _const_spec((1, LANES)),
            _const_spec((2 * LANES, 2 * LANES)),
        ],
        out_specs=(
            pl.BlockSpec((1, ATTN_WIDTH, tm), lambda i, j: (i, 0, j)),
            pl.BlockSpec((1, N_KV_HEADS, tm, LANES), lambda i, j: (i, 0, j, 0)),
            pl.BlockSpec((1, KV_WIDTH, tm), lambda i, j: (i, 0, j)),
            pl.BlockSpec((1, four_tile, four_width), lambda i, j: (i, j, 0)),
            pl.BlockSpec((1, tm, 2 * d), lambda i, j: (i, j, 0)),
        ),
        out_shape=out_shape,
        scratch_shapes=scratch,
        compiler_params=_params(2),
        name="in_proj",
    )(x, norm_g, w_in, cos, sin, qa, qb, gk, kbias, hmean)


def _attn_kernel(qt_ref, k_ref, vt_ref, km_ref, vtm_ref, ot_ref, *, tk, bounded):
    tq = qt_ref.shape[2]
    n_keys = k_ref.shape[2]
    n_blocks = n_keys // tk
    heads = range(Q_PER_KV)

    def head_qt(h, lo, width):
        ones_row = (lax.broadcasted_iota(jnp.int32, (HEAD_DIM, width), 0) == 0).astype(BF16)
        return jnp.concatenate([qt_ref[0, h * HEAD_DIM:(h + 1) * HEAD_DIM, lo:lo + width], ones_row], axis=0)

    if bounded:
        sub = min(tq, BOUNDED_Q_SUBTILE)
        for lo in range(0, tq, sub):
            qts = [head_qt(h, lo, sub) for h in heads]
            p = [jnp.exp2(_dot(km_ref[0, 0], qts[h])) for h in heads]
            l = [jnp.sum(p[h], axis=0, keepdims=True) for h in heads]
            acc = [_dot(vtm_ref[0], p[h].astype(BF16)) for h in heads]
            s = [_dot(k_ref[0, 0, 0:tk, :], qts[h]) for h in heads]
            for j in range(n_blocks):
                for h in heads:
                    ph = jnp.exp2(s[h])
                    l[h] = l[h] + jnp.sum(ph, axis=0, keepdims=True)
                    if j + 1 < n_blocks:
                        s[h] = _dot(k_ref[0, 0, (j + 1) * tk:(j + 2) * tk, :], qts[h])
                    acc[h] = acc[h] + _dot(vt_ref[0, :, j * tk:(j + 1) * tk], ph.astype(BF16))
            for h in heads:
                ot_ref[0, h * HEAD_DIM:(h + 1) * HEAD_DIM, lo:lo + sub] = (acc[h] / l[h]).astype(BF16)
        return

    qt = jnp.concatenate([head_qt(h, 0, tq) for h in heads], axis=1)

    def block(kb, vb, m, l, acc):
        s = _dot(kb, qt)
        m_new = jnp.maximum(m, jnp.max(s, axis=0, keepdims=True))
        alpha = jnp.exp2(m - m_new)
        p = jnp.exp2(s - m_new)
        return (m_new, alpha * l + jnp.sum(p, axis=0, keepdims=True), alpha * acc + _dot(vb, p.astype(BF16)))

    width = qt.shape[1]
    init = (jnp.full((1, width), -jnp.inf, F32), jnp.zeros((1, width), F32), jnp.zeros((HEAD_DIM, width), F32))
    carry = block(km_ref[0, 0], vtm_ref[0], *init)

    def body(j, carry):
        start = pl.multiple_of(j * tk, tk)
        return block(k_ref[0, 0, pl.ds(start, tk), :], vt_ref[0, :, pl.ds(start, tk)], *carry)

    _, l, acc = lax.fori_loop(0, n_blocks, body, carry)
    out = (acc / l).astype(BF16)
    for h in range(Q_PER_KV):
        ot_ref[0, h * HEAD_DIM:(h + 1) * HEAD_DIM, :] = out[:, h * tq:(h + 1) * tq]


def _attention(qt, k, vt, k_meta, vt_meta, tq, tk, bounded):
    b, _, n = qt.shape
    group_rows = Q_PER_KV * HEAD_DIM
    return pl.pallas_call(
        functools.partial(_attn_kernel, tk=tk, bounded=bounded),
        grid=(b, N_KV_HEADS, n // tq),
        in_specs=[
            pl.BlockSpec((1, group_rows, tq), lambda i, g, j: (i, g, j)),
            pl.BlockSpec((1, 1, n, LANES), lambda i, g, j: (i, g, 0, 0)),
            pl.BlockSpec((1, HEAD_DIM, n), lambda i, g, j: (i, g, 0)),
            pl.BlockSpec((1, 1, N_META, LANES), lambda i, g, j: (0, g, 0, 0)),
            pl.BlockSpec((1, HEAD_DIM, N_META), lambda i, g, j: (0, g, 0)),
        ],
        out_specs=pl.BlockSpec((1, group_rows, tq), lambda i, g, j: (i, g, j)),
        out_shape=jax.ShapeDtypeStruct((b, ATTN_WIDTH, n), BF16),
        compiler_params=_params(3),
        name="attention",
    )(qt, k, vt, k_meta, vt_meta)


def _fft_a_kernel(e_ref, x_ref, t_ref, xm_ref, out_ref, meta_ref):
    n1p = out_ref.shape[1]
    width = FOURIER_WIDTH
    residues = e_ref.shape[0]

    @pl.when(pl.program_id(1) == 0)
    def _():
        for q in range(residues):
            meta_ref[q] = _dot(t_ref[q], xm_ref[...])

    for q in range(residues):
        r = _dot(e_ref[q], x_ref[0, :, q * width:(q + 1) * width]) + meta_ref[q]
        for g in range(N_FOURIER_GROUPS):
            base = (q * N_FOURIER_GROUPS + g) * 2 * FOURIER_GROUP
            cols = slice(g * FOURIER_GROUP, (g + 1) * FOURIER_GROUP)
            out_ref[0, :, base:base + FOURIER_GROUP] = r[:n1p, cols].astype(BF16)
            out_ref[0, :, base + FOURIER_GROUP:base + 2 * FOURIER_GROUP] = r[n1p:, cols].astype(BF16)


def _times_unit(re, im, turn):
    quarter = turn * 4.0
    if quarter == round(quarter):
        return [(re, im), (im, -re), (-re, -im), (-im, re)][int(round(quarter)) % 4]
    c, s = math.cos(2.0 * math.pi * turn), -math.sin(2.0 * math.pi * turn)
    return re * c - im * s, re * s + im * c


def _fft_real_part(xs):
    def fft(seq):
        n = len(seq)
        if n == 1:
            return seq
        even, odd = fft(seq[0::2]), fft(seq[1::2])
        out = [None] * n
        for k in range(n // 2):
            tr, ti = _times_unit(*odd[k], k / n)
            out[k] = (even[k][0] + tr, even[k][1] + ti)
            out[k + n // 2] = (even[k][0] - tr, even[k][1] - ti)
        return out

    n = len(xs)
    even, odd = fft(xs[0::2]), fft(xs[1::2])
    out = [None] * n
    for k in range(n // 2):
        tr, _ = _times_unit(*odd[k], k / n)
        out[k] = even[k][0] + tr
        out[k + n // 2] = even[k][0] - tr
    return out


def _fft_b_kernel(ab_ref, mc_ref, out_ref):
    rows = ab_ref.shape[1]
    per_residue = N_FOURIER_GROUPS * 2 * FOURIER_GROUP
    for g in range(N_FOURIER_GROUPS):
        lo = g * 2 * FOURIER_GROUP
        lhs = jnp.concatenate([ab_ref[0, :, q * per_residue + lo:q * per_residue + lo + 2 * FOURIER_GROUP]
                               for q in range(FFT_RADIX)], axis=0)
        c = _dot(lhs, mc_ref[...])
        xs = [(c[q * rows:(q + 1) * rows, :FOURIER_GROUP], c[q * rows:(q + 1) * rows, FOURIER_GROUP:])
              for q in range(FFT_RADIX)]
        for k2, y in enumerate(_fft_real_part(xs)):
            out_ref[0, k2, :, g * FOURIER_GROUP:(g + 1) * FOURIER_GROUP] = y.astype(BF16)


def _fft_row_tile(n1p):
    return max(t for t in range(16, 193, 16) if n1p % t == 0)


def _seq_fft(xv, f_meta, e, t, mc):
    b, m, _ = xv.shape
    width = FOURIER_WIDTH
    n1p = e.shape[1] // 2
    qs = FFT_RADIX if 2 * n1p * m <= FFT_A_SMALL_MATRIX else FFT_A_RESIDUES_PER_STEP
    per_residue = N_FOURIER_GROUPS * 2 * FOURIER_GROUP
    ab = pl.pallas_call(
        _fft_a_kernel,
        grid=(FFT_RADIX // qs, b),
        in_specs=[
            pl.BlockSpec((qs, 2 * n1p, m), lambda q, i: (q, 0, 0)),
            pl.BlockSpec((1, m, qs * width), lambda q, i: (i, 0, q)),
            pl.BlockSpec((qs, 2 * n1p, FFT_RADIX), lambda q, i: (q, 0, 0)),
            pl.BlockSpec((FFT_RADIX, width), lambda q, i: (0, 0)),
        ],
        out_specs=pl.BlockSpec((1, n1p, qs * per_residue), lambda q, i: (i, 0, q)),
        out_shape=jax.ShapeDtypeStruct((b, n1p, FFT_RADIX * per_residue), BF16),
        scratch_shapes=[pltpu.VMEM((qs, 2 * n1p, width), F32)],
        compiler_params=_params(2),
        name="seq_fft_a",
    )(e, xv, t, f_meta)
    tr = _fft_row_tile(n1p)
    return pl.pallas_call(
        _fft_b_kernel,
        grid=(b, n1p // tr),
        in_specs=[
            pl.BlockSpec((1, tr, FFT_RADIX * per_residue), lambda i, r: (i, r, 0)),
            _const_spec(mc.shape),
        ],
        out_specs=pl.BlockSpec((1, FFT_RADIX, tr, width), lambda i, r: (i, 0, r, 0)),
        out_shape=jax.ShapeDtypeStruct((b, FFT_RADIX, n1p, width), BF16),
        compiler_params=_params(2),
        name="seq_fft_b",
    )(ab, mc)


def _seq_fft_tables(n):
    length = n + N_META
    m = n // FFT_RADIX
    n1 = m + 1
    n1p = -(-n1 // 16) * 16
    k1 = jnp.arange(n1p, dtype=jnp.int32)
    n2 = jnp.arange(FFT_RADIX, dtype=jnp.int32)
    valid = (k1 < n1).astype(F32)
    c1, s1 = _unit_circle(k1[:, None] * (FFT_RADIX * (jnp.arange(m, dtype=jnp.int32) + 1))[None, :], length)
    c2, s2 = _unit_circle(n2[:, None] * k1[None, :], length)
    c2, s2 = c2 * valid[None, :], s2 * valid[None, :]
    cos = c1[None, :, :] * c2[:, :, None] - s1[None, :, :] * s2[:, :, None]
    sin = s1[None, :, :] * c2[:, :, None] + c1[None, :, :] * s2[:, :, None]
    e = jnp.concatenate([cos, -sin], axis=1).astype(BF16)
    pick = (n2[:, None, None] == n2[None, None, :]).astype(F32)
    t = jnp.concatenate([c2[:, :, None] * pick, -s2[:, :, None] * pick], axis=1).astype(BF16)
    ch = jnp.arange(FOURIER_GROUP, dtype=jnp.int32)
    cb, sb = _unit_circle(ch[:, None] * ch[None, :], FOURIER_GROUP)
    scale = 1.0 / math.sqrt(length * FOURIER_GROUP)
    mc = jnp.concatenate([jnp.concatenate([cb, -sb], axis=1), jnp.concatenate([sb, cb], axis=1)], axis=0) * scale
    return e, t, mc.astype(BF16)


def _rms(x, g):
    return x * lax.rsqrt(jnp.mean(x * x, axis=-1, keepdims=True) + EPS) * g


def _slab_rows(slabs_ref, first_pos, tm, n1):
    n1p = slabs_ref.shape[2]
    window = min(FFT_RADIX, -(-(tm - 1) // n1) + 1)
    span = min(n1p, -(-(tm + 15) // 16) * 16)
    first_slab = first_pos // n1
    rel0 = first_pos - first_slab * n1
    row0 = pl.multiple_of(jnp.minimum((rel0 // 16) * 16, n1p - span), 16)
    rel = lax.broadcasted_iota(jnp.int32, (tm, 1), 0) + rel0
    target = rel - row0
    for w in range(1, window):
        target = target + jnp.where(rel >= w * n1, span - n1 + (row0 if w == 1 else 0), 0)
    col = lax.broadcasted_iota(jnp.int32, (tm, window * span), 1)
    pick = (col == target).astype(BF16)
    rows = jnp.concatenate([slabs_ref[0, first_slab, pl.ds(row0, span), :]]
                           + [slabs_ref[0, jnp.minimum(first_slab + w, FFT_RADIX - 1), 0:span, :]
                              for w in range(1, window)], axis=0)
    return _dot(pick, rows).astype(BF16)


def _mix_ffn_kernel(x_ref, ot_ref, gate_ref, slabs_ref, wao_ref, wfo_ref, wout_ref, g2_ref, wfi_ref, wfo2_ref,
                    gfin_ref, y_ref, *, ff_chunk, slab_n1):
    tm, d_model = x_ref.shape[1:]
    d_ff = wfo2_ref.shape[0]
    sub = min(tm, MIX_FFN_SUBTILE)
    spans = [slice(r, r + sub) for r in range(0, tm, sub)]
    a, fo, x1, h2 = [], [], [], []
    for rows in spans:
        attn = ot_ref[0, :, rows].astype(F32).T.astype(BF16)
        a.append(_dot(attn, wao_ref[...]))
        four = _slab_rows(slabs_ref, N_META + pl.program_id(1) * tm + rows.start, sub, slab_n1)
        fo.append(_dot(four, wfo_ref[...]))
    for s, rows in enumerate(spans):
        gates = gate_ref[0, rows, :]
        merged = gates[:, :d_model].astype(F32) * a[s] + gates[:, d_model:].astype(F32) * fo[s]
        x1.append(x_ref[0, rows, :] + _dot(merged.astype(BF16), wout_ref[...]))
    for s in range(len(spans)):
        h2.append(_rms(x1[s], g2_ref[...]).astype(BF16))
    for s, rows in enumerate(spans):
        acc = x1[s]
        for c in range(d_ff // ff_chunk):
            gt = _dot(h2[s], wfi_ref[:, c * ff_chunk:(c + 1) * ff_chunk])
            up = _dot(h2[s], wfi_ref[:, d_ff + c * ff_chunk:d_ff + (c + 1) * ff_chunk])
            act = gt * (1.0 / (1.0 + jnp.exp(-gt))) * up
            acc = acc + _dot(act.astype(BF16), wfo2_ref[c * ff_chunk:(c + 1) * ff_chunk, :])
        y_ref[0, rows, :] = _rms(acc, gfin_ref[...])


def _mix_ffn(x, ot, four, gates, w_attn_o, w_four_o, w_out, g2, w_ffn_in, w_ffn_out, gfin, tm, ff_chunk):
    b, n, d = x.shape
    d_ff = w_ffn_out.shape[0]
    slab_n1 = n // FFT_RADIX + 1
    four_spec = pl.BlockSpec((1,) + four.shape[1:], lambda i, j: (i, 0, 0, 0), pipeline_mode=pl.Buffered(1))
    return pl.pallas_call(
        functools.partial(_mix_ffn_kernel, ff_chunk=ff_chunk, slab_n1=slab_n1),
        grid=(b, n // tm),
        in_specs=[
            pl.BlockSpec((1, tm, d), lambda i, j: (i, j, 0)),
            pl.BlockSpec((1, ATTN_WIDTH, tm), lambda i, j: (i, 0, j)),
            pl.BlockSpec((1, tm, 2 * d), lambda i, j: (i, j, 0)),
            four_spec,
            _const_spec((ATTN_WIDTH, d)),
            _const_spec((FOURIER_WIDTH, d)),
            _const_spec((d, d)),
            _const_spec((1, d)),
            _const_spec((d, 2 * d_ff)),
            _const_spec((d_ff, d)),
            _const_spec((1, d)),
        ],
        out_specs=pl.BlockSpec((1, tm, d), lambda i, j: (i, j, 0)),
        out_shape=jax.ShapeDtypeStruct((b, n, d), F32),
        compiler_params=_params(2),
        name="mix_ffn",
    )(x, ot, gates, four, w_attn_o, w_four_o, w_out, g2, w_ffn_in, w_ffn_out, gfin)


def _rope_tables(row, col):
    inv_freq = 1.0 / (ROPE_THETA ** (jnp.arange(0, ROPE_AXIS_DIM, 2, dtype=F32) / ROPE_AXIS_DIM))
    ar = row[:, None] * inv_freq[None, :]
    ac = col[:, None] * inv_freq[None, :]
    cos = jnp.concatenate([jnp.cos(ar), jnp.cos(ar), jnp.cos(ac), jnp.cos(ac)], axis=-1)
    sin = jnp.concatenate([-jnp.sin(ar), jnp.sin(ar), -jnp.sin(ac), jnp.sin(ac)], axis=-1)
    reps = LANES // HEAD_DIM
    return jnp.tile(cos, (1, reps)), jnp.tile(sin, (1, reps))


def _q_rope_tables(cos, sin, gq):
    partner = jnp.arange(HEAD_DIM) ^ (ROPE_AXIS_DIM // 2)
    qa = (cos[:, :HEAD_DIM] * gq[None, :] * Q_SCALE).T
    qb = (sin[:, :HEAD_DIM] * gq[partner][None, :] * Q_SCALE).T
    return qa, qb


def _unit_circle(prod, length):
    ang = (prod % length).astype(F32) * (2.0 * math.pi / length)
    return jnp.cos(ang), jnp.sin(ang)


def _encode(x, meta_tokens, p):
    b, n, d = x.shape
    tok = jnp.arange(n, dtype=jnp.int32)
    cos, sin = _rope_tables((tok // GRID_W).astype(F32), (tok % GRID_W).astype(F32))
    meta_pad = LANES
    mrow = jnp.full((meta_pad,), -1.0, F32)
    mcol = jnp.arange(meta_pad, dtype=F32)
    cos_m, sin_m = _rope_tables(mrow, mcol)
    x_meta = jnp.zeros((1, meta_pad, d), F32).at[0, :N_META].set(meta_tokens)

    consts = (p["norm_mix_g"], p["w_in"])
    head_consts = (p["gk"], p["kbias"], p["hmean"])
    _, k_m, vt_m, f_m, _ = _in_proj(x_meta, *consts, cos_m, sin_m, *_q_rope_tables(cos_m, sin_m, p["gq"]),
                                    *head_consts, tm=meta_pad, residue_major=False)
    qt, k, vt, xv, gates = _in_proj(x, *consts, cos, sin, *_q_rope_tables(cos, sin, p["gq"]), *head_consts,
                                    tm=min(n, IN_PROJ_TILE), residue_major=True)

    attn = functools.partial(_attention, qt, k, vt, k_m[:, :, :N_META], vt_m[:, :, :N_META])
    ot = lax.cond(p["bounded"],
                  functools.partial(attn, tq=min(n, 2 * BOUNDED_Q_SUBTILE), tk=min(n, 256), bounded=True),
                  functools.partial(attn, tq=min(n, 256), tk=min(n, 512), bounded=False))

    four = _seq_fft(xv, f_m[0, :N_META], *_seq_fft_tables(n))

    return _mix_ffn(x, ot, four, gates, p["w_attn_o"], p["w_four_o"], p["w_out"], p["norm_ffn_g"],
                    p["w_ffn_in"], p["w_ffn_out"], p["final_norm_g"], tm=min(n, MIX_FFN_TILE),
                    ff_chunk=p["w_ffn_out"].shape[0])


def kernel(x_prompt, x_sample, meta_tokens, norm_mix_g, w_in, q_norm_g, k_norm_g, w_attn_o, w_four_o, w_out,
           norm_ffn_g, w_ffn_in, w_ffn_out, final_norm_g):
    assert w_in.shape[0] == 1, "single-layer encoder: meta rows are only needed as keys / DFT inputs"
    reps = LANES // HEAD_DIM
    head_id = jnp.arange(2 * LANES) // HEAD_DIM
    bound = 1.01 * LOG2E * HEAD_DIM ** 0.5 * jnp.max(jnp.abs(q_norm_g[0])) * jnp.max(jnp.abs(k_norm_g[0]))
    bounded = bound <= MAX_SCORE_BOUND_LOG2
    p = {
        "bounded": bounded,
        "kbias": jnp.where((jnp.arange(LANES) == HEAD_DIM) & bounded, -bound, 0.0).astype(F32)[None, :],
        "norm_mix_g": norm_mix_g[0][None, :],
        "w_in": w_in[0].astype(BF16),
        "gq": q_norm_g[0],
        "gk": jnp.tile(k_norm_g[0], reps)[None, :],
        "hmean": ((head_id[:, None] == head_id[None, :]).astype(F32) / HEAD_DIM).astype(BF16),
        "w_attn_o": w_attn_o[0].astype(BF16),
        "w_four_o": w_four_o[0].astype(BF16),
        "w_out": w_out[0].astype(BF16),
        "norm_ffn_g": norm_ffn_g[0][None, :],
        "w_ffn_in": w_ffn_in[0].astype(BF16),
        "w_ffn_out": w_ffn_out[0].astype(BF16),
        "final_norm_g": final_norm_g[None, :],
    }
    return (_encode(x_prompt, meta_tokens, p), _encode(x_sample, meta_tokens, p))
```

```python
import functools
import math

import jax
import jax.numpy as jnp
from jax import lax
from jax.experimental import pallas as pl
from jax.experimental.pallas import tpu as pltpu

N_META = 16
GRID_W = 64
HEAD_DIM = 64
N_Q_HEADS = 8
N_KV_HEADS = 2
Q_PER_KV = N_Q_HEADS // N_KV_HEADS
ATTN_WIDTH = N_Q_HEADS * HEAD_DIM
KV_WIDTH = N_KV_HEADS * HEAD_DIM
FOURIER_WIDTH = 512
FOURIER_GROUP = 128
N_FOURIER_GROUPS = FOURIER_WIDTH // FOURIER_GROUP
ROPE_AXIS_DIM = HEAD_DIM // 2
ROPE_THETA = 10000.0
EPS = 1e-6

LANES = 128
VMEM_LIMIT_BYTES = 56 * 1024 * 1024

F32 = jnp.float32
BF16 = jnp.bfloat16

LOG2E = math.log2(math.e)
Q_SCALE = HEAD_DIM ** -0.5 * LOG2E
MAX_SCORE_BOUND_LOG2 = 60.0
BOUNDED_Q_SUBTILE = 2 * LANES
BOUNDED_KEY_BLOCK = 2 * LANES
BOUNDED_UNROLLED_TILES = 64

FFT_RADIX = N_META
FFT_A_RESIDUES_PER_STEP = 4
FFT_A_SMALL_MATRIX = 1 << 16

IN_PROJ_TILE = 512
IN_PROJ_SUBTILE = 512
MIX_FFN_TILE = 512
MIX_FFN_SUBTILE = 256


def _dot(a, b):
    return jnp.dot(a, b, preferred_element_type=F32)


def _const_spec(shape):
    zeros = (0,) * len(shape)
    return pl.BlockSpec(shape, lambda *_: zeros, pipeline_mode=pl.Buffered(1))


def _params(n_axes):
    return pltpu.CompilerParams(dimension_semantics=("arbitrary",) * n_axes,
                                vmem_limit_bytes=VMEM_LIMIT_BYTES)


def _in_proj_kernel(x_ref, g_ref, w_ref, cos_ref, sin_ref, qa_ref, qb_ref, gk_ref, kbias_ref, hmean_ref,
                    qt_ref, k_ref, vt_ref, z_ref, gate_ref, *slab_ref):
    tm, d_model = x_ref.shape[1:]
    sub = min(tm, IN_PROJ_SUBTILE)
    o0 = ATTN_WIDTH
    o1 = o0 + KV_WIDTH
    o2 = o1 + KV_WIDTH
    o3 = o2 + FOURIER_WIDTH
    lane = lax.broadcasted_iota(jnp.int32, (sub, LANES), 1)
    first_half = (lane % ROPE_AXIS_DIM) < (ROPE_AXIS_DIM // 2)
    low = lane < HEAD_DIM

    def head_norm(z2):
        return z2 * lax.rsqrt(_dot((z2 * z2).astype(BF16), hmean_ref[...]) + EPS)

    spans = [slice(r, r + sub) for r in range(0, tm, sub)]
    subs = range(len(spans))
    hs = []
    for rows in spans:
        x = x_ref[0, rows, :]
        hs.append((x * lax.rsqrt(jnp.mean(x * x, axis=-1, keepdims=True) + EPS) * g_ref[...]).astype(BF16))

    def rope(y, rows):
        half = ROPE_AXIS_DIM // 2
        partner = jnp.where(first_half, pltpu.roll(y, LANES - half, 1), pltpu.roll(y, half, 1))
        return y * cos_ref[rows, :] + partner * sin_ref[rows, :]

    zq = [_dot(hs[s], w_ref[:, :o0]) for s in subs]
    quarter = ROPE_AXIS_DIM // 2
    for s, rows in enumerate(spans):
        for c in range(ATTN_WIDTH // LANES):
            zt = zq[s][:, c * LANES:(c + 1) * LANES].T
            for hh in range(LANES // HEAD_DIM):
                blk = zt[hh * HEAD_DIM:(hh + 1) * HEAD_DIM, :]
                y = blk * lax.rsqrt(jnp.mean(blk * blk, axis=0, keepdims=True) + EPS)
                partner = jnp.concatenate([y[quarter:2 * quarter], y[:quarter], y[3 * quarter:], y[2 * quarter:3 * quarter]],
                                          axis=0)
                r0 = c * LANES + hh * HEAD_DIM
                qt_ref[0, r0:r0 + HEAD_DIM, rows] = (y * qa_ref[:, rows] + partner * qb_ref[:, rows]).astype(BF16)

    zkv = [_dot(hs[s], w_ref[:, o0:o2]) for s in subs]
    kbias = kbias_ref[...]
    for s, rows in enumerate(spans):
        yk = rope(head_norm(zkv[s])[:, :KV_WIDTH] * gk_ref[...], rows)
        k_ref[0, 0, rows, :] = jnp.where(low, yk, kbias).astype(BF16)
        k_ref[0, 1, rows, :] = jnp.where(low, pltpu.roll(yk, HEAD_DIM, 1), kbias).astype(BF16)
        vt_ref[0, :, rows] = zkv[s][:, KV_WIDTH:].T.astype(BF16)

    for s, rows in enumerate(spans):
        zf = _dot(hs[s], w_ref[:, o2:o3])
        if not slab_ref:
            z_ref[0, rows, :] = zf.astype(BF16)
        else:
            slabs, = slab_ref
            n_rows = sub // FFT_RADIX
            for g in range(N_FOURIER_GROUPS):
                slabs[s, g] = zf[:, g * FOURIER_GROUP:(g + 1) * FOURIER_GROUP]
            for j in range(FFT_RADIX):
                for g in range(N_FOURIER_GROUPS):
                    lo = j * FOURIER_WIDTH + g * FOURIER_GROUP
                    z_ref[0, s * n_rows:(s + 1) * n_rows, lo:lo + FOURIER_GROUP] = (
                        slabs[s, g, pl.ds(j, n_rows, stride=FFT_RADIX), :].astype(BF16))

    gate_chunk = 512
    for c in range(2 * d_model // gate_chunk):
        for s, rows in enumerate(spans):
            zg = _dot(hs[s], w_ref[:, o3 + c * gate_chunk:o3 + (c + 1) * gate_chunk])
            gate_ref[0, rows, c * gate_chunk:(c + 1) * gate_chunk] = (1.0 / (1.0 + jnp.exp(-zg))).astype(BF16)


def _in_proj(x, norm_g, w_in, cos, sin, qa, qb, gk, kbias, hmean, tm, residue_major):
    b, n, d = x.shape
    in_width = w_in.shape[1]
    grid = (b, n // tm)
    if residue_major:
        four_rows, four_tile, four_width = n // FFT_RADIX, tm // FFT_RADIX, FFT_RADIX * FOURIER_WIDTH
        sub = min(tm, IN_PROJ_SUBTILE)
        scratch = [pltpu.VMEM((tm // sub, N_FOURIER_GROUPS, sub, FOURIER_GROUP), F32)]
    else:
        four_rows, four_tile, four_width = n, tm, FOURIER_WIDTH
        scratch = []
    out_shape = (
        jax.ShapeDtypeStruct((b, ATTN_WIDTH, n), BF16),
        jax.ShapeDtypeStruct((b, N_KV_HEADS, n, LANES), BF16),
        jax.ShapeDtypeStruct((b, KV_WIDTH, n), BF16),
        jax.ShapeDtypeStruct((b, four_rows, four_width), BF16),
        jax.ShapeDtypeStruct((b, n, 2 * d), BF16),
    )
    return pl.pallas_call(
        _in_proj_kernel,
        grid=grid,
        in_specs=[
            pl.BlockSpec((1, tm, d), lambda i, j: (i, j, 0)),
            _const_spec((1, d)),
            _const_spec((d, in_width)),
            pl.BlockSpec((tm, LANES), lambda i, j: (j, 0)),
            pl.BlockSpec((tm, LANES), lambda i, j: (j, 0)),
            pl.BlockSpec((HEAD_DIM, tm), lambda i, j: (0, j)),
            pl.BlockSpec((HEAD_DIM, tm), lambda i, j: (0, j)),
            _const_spec((1, LANES)),
            _const_spec((1, LANES)),
            _const_spec((2 * LANES, 2 * LANES)),
        ],
        out_specs=(
            pl.BlockSpec((1, ATTN_WIDTH, tm), lambda i, j: (i, 0, j)),
            pl.BlockSpec((1, N_KV_HEADS, tm, LANES), lambda i, j: (i, 0, j, 0)),
            pl.BlockSpec((1, KV_WIDTH, tm), lambda i, j: (i, 0, j)),
            pl.BlockSpec((1, four_tile, four_width), lambda i, j: (i, j, 0)),
            pl.BlockSpec((1, tm, 2 * d), lambda i, j: (i, j, 0)),
        ),
        out_shape=out_shape,
        scratch_shapes=scratch,
        compiler_params=_params(2),
        name="in_proj",
    )(x, norm_g, w_in, cos, sin, qa, qb, gk, kbias, hmean)


def _attn_kernel(qt_ref, k_ref, vt_ref, km_ref, vtm_ref, ot_ref, *, tk, bounded):
    tq = qt_ref.shape[2]
    n_keys = k_ref.shape[2]
    n_blocks = n_keys // tk
    heads = range(Q_PER_KV)

    def head_qt(h, lo, width):
        ones_row = (lax.broadcasted_iota(jnp.int32, (HEAD_DIM, width), 0) == 0).astype(BF16)
        return jnp.concatenate([qt_ref[0, h * HEAD_DIM:(h + 1) * HEAD_DIM, lo:lo + width], ones_row], axis=0)

    if bounded:
        sub = min(tq, BOUNDED_Q_SUBTILE)
        for lo in range(0, tq, sub):
            qts = [head_qt(h, lo, sub) for h in heads]
            p = [jnp.exp2(_dot(km_ref[0, 0], qts[h])) for h in heads]
            l = [jnp.sum(p[h], axis=0, keepdims=True) for h in heads]
            acc = [_dot(vtm_ref[0], p[h].astype(BF16)) for h in heads]
            s = [_dot(k_ref[0, 0, 0:tk, :], qts[h]) for h in heads]
            for j in range(n_blocks):
                for h in heads:
                    ph = jnp.exp2(s[h])
                    l[h] = l[h] + jnp.sum(ph, axis=0, keepdims=True)
                    if j + 1 < n_blocks:
                        s[h] = _dot(k_ref[0, 0, (j + 1) * tk:(j + 2) * tk, :], qts[h])
                    acc[h] = acc[h] + _dot(vt_ref[0, :, j * tk:(j + 1) * tk], ph.astype(BF16))
            for h in heads:
                ot_ref[0, h * HEAD_DIM:(h + 1) * HEAD_DIM, lo:lo + sub] = (acc[h] / l[h]).astype(BF16)
        return

    qt = jnp.concatenate([head_qt(h, 0, tq) for h in heads], axis=1)

    def block(kb, vb, m, l, acc):
        s = _dot(kb, qt)
        m_new = jnp.maximum(m, jnp.max(s, axis=0, keepdims=True))
        alpha = jnp.exp2(m - m_new)
        p = jnp.exp2(s - m_new)
        return (m_new, alpha * l + jnp.sum(p, axis=0, keepdims=True), alpha * acc + _dot(vb, p.astype(BF16)))

    width = qt.shape[1]
    init = (jnp.full((1, width), -jnp.inf, F32), jnp.zeros((1, width), F32), jnp.zeros((HEAD_DIM, width), F32))
    carry = block(km_ref[0, 0], vtm_ref[0], *init)

    def body(j, carry):
        start = pl.multiple_of(j * tk, tk)
        return block(k_ref[0, 0, pl.ds(start, tk), :], vt_ref[0, :, pl.ds(start, tk)], *carry)

    _, l, acc = lax.fori_loop(0, n_blocks, body, carry)
    out = (acc / l).astype(BF16)
    for h in range(Q_PER_KV):
        ot_ref[0, h * HEAD_DIM:(h + 1) * HEAD_DIM, :] = out[:, h * tq:(h + 1) * tq]


def _attention(qt, k, vt, k_meta, vt_meta, tq, tk, bounded):
    b, _, n = qt.shape
    group_rows = Q_PER_KV * HEAD_DIM
    return pl.pallas_call(
        functools.partial(_attn_kernel, tk=tk, bounded=bounded),
        grid=(b, N_KV_HEADS, n // tq),
        in_specs=[
            pl.BlockSpec((1, group_rows, tq), lambda i, g, j: (i, g, j)),
            pl.BlockSpec((1, 1, n, LANES), lambda i, g, j: (i, g, 0, 0)),
            pl.BlockSpec((1, HEAD_DIM, n), lambda i, g, j: (i, g, 0)),
            pl.BlockSpec((1, 1, N_META, LANES), lambda i, g, j: (0, g, 0, 0)),
            pl.BlockSpec((1, HEAD_DIM, N_META), lambda i, g, j: (0, g, 0)),
        ],
        out_specs=pl.BlockSpec((1, group_rows, tq), lambda i, g, j: (i, g, j)),
        out_shape=jax.ShapeDtypeStruct((b, ATTN_WIDTH, n), BF16),
        compiler_params=_params(3),
        name="attention",
    )(qt, k, vt, k_meta, vt_meta)


def _fft_a_kernel(e_ref, x_ref, t_ref, xm_ref, out_ref, meta_ref):
    n1p = out_ref.shape[1]
    width = FOURIER_WIDTH
    residues = e_ref.shape[0]

    @pl.when(pl.program_id(1) == 0)
    def _():
        for q in range(residues):
            meta_ref[q] = _dot(t_ref[q], xm_ref[...])

    for q in range(residues):
        r = _dot(e_ref[q], x_ref[0, :, q * width:(q + 1) * width]) + meta_ref[q]
        for g in range(N_FOURIER_GROUPS):
            base = (q * N_FOURIER_GROUPS + g) * 2 * FOURIER_GROUP
            cols = slice(g * FOURIER_GROUP, (g + 1) * FOURIER_GROUP)
            out_ref[0, :, base:base + FOURIER_GROUP] = r[:n1p, cols].astype(BF16)
            out_ref[0, :, base + FOURIER_GROUP:base + 2 * FOURIER_GROUP] = r[n1p:, cols].astype(BF16)


def _times_unit(re, im, turn):
    quarter = turn * 4.0
    if quarter == round(quarter):
        return [(re, im), (im, -re), (-re, -im), (-im, re)][int(round(quarter)) % 4]
    c, s = math.cos(2.0 * math.pi * turn), -math.sin(2.0 * math.pi * turn)
    return re * c - im * s, re * s + im * c


def _fft_real_part(xs):
    def fft(seq):
        n = len(seq)
        if n == 1:
            return seq
        even, odd = fft(seq[0::2]), fft(seq[1::2])
        out = [None] * n
        for k in range(n // 2):
            tr, ti = _times_unit(*odd[k], k / n)
            out[k] = (even[k][0] + tr, even[k][1] + ti)
            out[k + n // 2] = (even[k][0] - tr, even[k][1] - ti)
        return out

    n = len(xs)
    even, odd = fft(xs[0::2]), fft(xs[1::2])
    out = [None] * n
    for k in range(n // 2):
        tr, _ = _times_unit(*odd[k], k / n)
        out[k] = even[k][0] + tr
        out[k + n // 2] = even[k][0] - tr
    return out


def _fft_b_kernel(ab_ref, mc_ref, out_ref):
    rows = ab_ref.shape[1]
    per_residue = N_FOURIER_GROUPS * 2 * FOURIER_GROUP
    for g in range(N_FOURIER_GROUPS):
        lo = g * 2 * FOURIER_GROUP
        lhs = jnp.concatenate([ab_ref[0, :, q * per_residue + lo:q * per_residue + lo + 2 * FOURIER_GROUP]
                               for q in range(FFT_RADIX)], axis=0)
        c = _dot(lhs, mc_ref[...])
        xs = [(c[q * rows:(q + 1) * rows, :FOURIER_GROUP], c[q * rows:(q + 1) * rows, FOURIER_GROUP:])
              for q in range(FFT_RADIX)]
        for k2, y in enumerate(_fft_real_part(xs)):
            out_ref[0, k2, :, g * FOURIER_GROUP:(g + 1) * FOURIER_GROUP] = y.astype(BF16)


def _fft_row_tile(n1p):
    return max(t for t in range(16, 193, 16) if n1p % t == 0)


def _seq_fft(xv, f_meta, e, t, mc):
    b, m, _ = xv.shape
    width = FOURIER_WIDTH
    n1p = e.shape[1] // 2
    qs = FFT_RADIX if 2 * n1p * m <= FFT_A_SMALL_MATRIX else FFT_A_RESIDUES_PER_STEP
    per_residue = N_FOURIER_GROUPS * 2 * FOURIER_GROUP
    ab = pl.pallas_call(
        _fft_a_kernel,
        grid=(FFT_RADIX // qs, b),
        in_specs=[
            pl.BlockSpec((qs, 2 * n1p, m), lambda q, i: (q, 0, 0)),
            pl.BlockSpec((1, m, qs * width), lambda q, i: (i, 0, q)),
            pl.BlockSpec((qs, 2 * n1p, FFT_RADIX), lambda q, i: (q, 0, 0)),
            pl.BlockSpec((FFT_RADIX, width), lambda q, i: (0, 0)),
        ],
        out_specs=pl.BlockSpec((1, n1p, qs * per_residue), lambda q, i: (i, 0, q)),
        out_shape=jax.ShapeDtypeStruct((b, n1p, FFT_RADIX * per_residue), BF16),
        scratch_shapes=[pltpu.VMEM((qs, 2 * n1p, width), F32)],
        compiler_params=_params(2),
        name="seq_fft_a",
    )(e, xv, t, f_meta)
    tr = _fft_row_tile(n1p)
    return pl.pallas_call(
        _fft_b_kernel,
        grid=(b, n1p // tr),
        in_specs=[
            pl.BlockSpec((1, tr, FFT_RADIX * per_residue), lambda i, r: (i, r, 0)),
            _const_spec(mc.shape),
        ],
        out_specs=pl.BlockSpec((1, FFT_RADIX, tr, width), lambda i, r: (i, 0, r, 0)),
        out_shape=jax.ShapeDtypeStruct((b, FFT_RADIX, n1p, width), BF16),
        compiler_params=_params(2),
        name="seq_fft_b",
    )(ab, mc)


def _seq_fft_tables(n):
    length = n + N_META
    m = n // FFT_RADIX
    n1 = m + 1
    n1p = -(-n1 // 16) * 16
    k1 = jnp.arange(n1p, dtype=jnp.int32)
    n2 = jnp.arange(FFT_RADIX, dtype=jnp.int32)
    valid = (k1 < n1).astype(F32)
    c1, s1 = _unit_circle(k1[:, None] * (FFT_RADIX * (jnp.arange(m, dtype=jnp.int32) + 1))[None, :], length)
    c2, s2 = _unit_circle(n2[:, None] * k1[None, :], length)
    c2, s2 = c2 * valid[None, :], s2 * valid[None, :]
    cos = c1[None, :, :] * c2[:, :, None] - s1[None, :, :] * s2[:, :, None]
    sin = s1[None, :, :] * c2[:, :, None] + c1[None, :, :] * s2[:, :, None]
    e = jnp.concatenate([cos, -sin], axis=1).astype(BF16)
    pick = (n2[:, None, None] == n2[None, None, :]).astype(F32)
    t = jnp.concatenate([c2[:, :, None] * pick, -s2[:, :, None] * pick], axis=1).astype(BF16)
    ch = jnp.arange(FOURIER_GROUP, dtype=jnp.int32)
    cb, sb = _unit_circle(ch[:, None] * ch[None, :], FOURIER_GROUP)
    scale = 1.0 / math.sqrt(length * FOURIER_GROUP)
    mc = jnp.concatenate([jnp.concatenate([cb, -sb], axis=1), jnp.concatenate([sb, cb], axis=1)], axis=0) * scale
    return e, t, mc.astype(BF16)


def _rms(x, g):
    return x * lax.rsqrt(jnp.mean(x * x, axis=-1, keepdims=True) + EPS) * g


def _slab_rows(slabs_ref, first_pos, tm, n1):
    n1p = slabs_ref.shape[2]
    window = min(FFT_RADIX, -(-(tm - 1) // n1) + 1)
    span = min(n1p, -(-(tm + 15) // 16) * 16)
    first_slab = first_pos // n1
    rel0 = first_pos - first_slab * n1
    row0 = pl.multiple_of(jnp.minimum((rel0 // 16) * 16, n1p - span), 16)
    rel = lax.broadcasted_iota(jnp.int32, (tm, 1), 0) + rel0
    target = rel - row0
    for w in range(1, window):
        target = target + jnp.where(rel >= w * n1, span - n1 + (row0 if w == 1 else 0), 0)
    col = lax.broadcasted_iota(jnp.int32, (tm, window * span), 1)
    pick = (col == target).astype(BF16)
    rows = jnp.concatenate([slabs_ref[0, first_slab, pl.ds(row0, span), :]]
                           + [slabs_ref[0, jnp.minimum(first_slab + w, FFT_RADIX - 1), 0:span, :]
                              for w in range(1, window)], axis=0)
    return _dot(pick, rows).astype(BF16)


def _mix_ffn_kernel(x_ref, ot_ref, gate_ref, slabs_ref, wao_ref, wfo_ref, wout_ref, g2_ref, wfi_ref, wfo2_ref,
                    gfin_ref, y_ref, *, ff_chunk, slab_n1):
    tm, d_model = x_ref.shape[1:]
    d_ff = wfo2_ref.shape[0]
    sub = min(tm, MIX_FFN_SUBTILE)
    spans = [slice(r, r + sub) for r in range(0, tm, sub)]
    a, fo, x1, h2 = [], [], [], []
    for rows in spans:
        attn = ot_ref[0, :, rows].astype(F32).T.astype(BF16)
        a.append(_dot(attn, wao_ref[...]))
        four = _slab_rows(slabs_ref, N_META + pl.program_id(1) * tm + rows.start, sub, slab_n1)
        fo.append(_dot(four, wfo_ref[...]))
    for s, rows in enumerate(spans):
        gates = gate_ref[0, rows, :]
        merged = gates[:, :d_model].astype(F32) * a[s] + gates[:, d_model:].astype(F32) * fo[s]
        x1.append(x_ref[0, rows, :] + _dot(merged.astype(BF16), wout_ref[...]))
    for s in range(len(spans)):
        h2.append(_rms(x1[s], g2_ref[...]).astype(BF16))
    for s, rows in enumerate(spans):
        acc = x1[s]
        for c in range(d_ff // ff_chunk):
            gt = _dot(h2[s], wfi_ref[:, c * ff_chunk:(c + 1) * ff_chunk])
            up = _dot(h2[s], wfi_ref[:, d_ff + c * ff_chunk:d_ff + (c + 1) * ff_chunk])
            act = gt * (1.0 / (1.0 + jnp.exp(-gt))) * up
            acc = acc + _dot(act.astype(BF16), wfo2_ref[c * ff_chunk:(c + 1) * ff_chunk, :])
        y_ref[0, rows, :] = _rms(acc, gfin_ref[...])


def _mix_ffn(x, ot, four, gates, w_attn_o, w_four_o, w_out, g2, w_ffn_in, w_ffn_out, gfin, tm, ff_chunk):
    b, n, d = x.shape
    d_ff = w_ffn_out.shape[0]
    slab_n1 = n // FFT_RADIX + 1
    four_spec = pl.BlockSpec((1,) + four.shape[1:], lambda i, j: (i, 0, 0, 0), pipeline_mode=pl.Buffered(1))
    return pl.pallas_call(
        functools.partial(_mix_ffn_kernel, ff_chunk=ff_chunk, slab_n1=slab_n1),
        grid=(b, n // tm),
        in_specs=[
            pl.BlockSpec((1, tm, d), lambda i, j: (i, j, 0)),
            pl.BlockSpec((1, ATTN_WIDTH, tm), lambda i, j: (i, 0, j)),
            pl.BlockSpec((1, tm, 2 * d), lambda i, j: (i, j, 0)),
            four_spec,
            _const_spec((ATTN_WIDTH, d)),
            _const_spec((FOURIER_WIDTH, d)),
            _const_spec((d, d)),
            _const_spec((1, d)),
            _const_spec((d, 2 * d_ff)),
            _const_spec((d_ff, d)),
            _const_spec((1, d)),
        ],
        out_specs=pl.BlockSpec((1, tm, d), lambda i, j: (i, j, 0)),
        out_shape=jax.ShapeDtypeStruct((b, n, d), F32),
        compiler_params=_params(2),
        name="mix_ffn",
    )(x, ot, gates, four, w_attn_o, w_four_o, w_out, g2, w_ffn_in, w_ffn_out, gfin)


def _rope_tables(row, col):
    inv_freq = 1.0 / (ROPE_THETA ** (jnp.arange(0, ROPE_AXIS_DIM, 2, dtype=F32) / ROPE_AXIS_DIM))
    ar = row[:, None] * inv_freq[None, :]
    ac = col[:, None] * inv_freq[None, :]
    cos = jnp.concatenate([jnp.cos(ar), jnp.cos(ar), jnp.cos(ac), jnp.cos(ac)], axis=-1)
    sin = jnp.concatenate([-jnp.sin(ar), jnp.sin(ar), -jnp.sin(ac), jnp.sin(ac)], axis=-1)
    reps = LANES // HEAD_DIM
    return jnp.tile(cos, (1, reps)), jnp.tile(sin, (1, reps))


def _q_rope_tables(cos, sin, gq):
    partner = jnp.arange(HEAD_DIM) ^ (ROPE_AXIS_DIM // 2)
    qa = (cos[:, :HEAD_DIM] * gq[None, :] * Q_SCALE).T
    qb = (sin[:, :HEAD_DIM] * gq[partner][None, :] * Q_SCALE).T
    return qa, qb


def _unit_circle(prod, length):
    ang = (prod % length).astype(F32) * (2.0 * math.pi / length)
    return jnp.cos(ang), jnp.sin(ang)


def _encode(x, meta_tokens, p):
    b, n, d = x.shape
    tok = jnp.arange(n, dtype=jnp.int32)
    cos, sin = _rope_tables((tok // GRID_W).astype(F32), (tok % GRID_W).astype(F32))
    meta_pad = LANES
    mrow = jnp.full((meta_pad,), -1.0, F32)
    mcol = jnp.arange(meta_pad, dtype=F32)
    cos_m, sin_m = _rope_tables(mrow, mcol)
    x_meta = jnp.zeros((1, meta_pad, d), F32).at[0, :N_META].set(meta_tokens)

    consts = (p["norm_mix_g"], p["w_in"])
    head_consts = (p["gk"], p["kbias"], p["hmean"])
    _, k_m, vt_m, f_m, _ = _in_proj(x_meta, *consts, cos_m, sin_m, *_q_rope_tables(cos_m, sin_m, p["gq"]),
                                    *head_consts, tm=meta_pad, residue_major=False)
    qt, k, vt, xv, gates = _in_proj(x, *consts, cos, sin, *_q_rope_tables(cos, sin, p["gq"]), *head_consts,
                                    tm=min(n, IN_PROJ_TILE), residue_major=True)

    attn = functools.partial(_attention, qt, k, vt, k_m[:, :, :N_META], vt_m[:, :, :N_META])
    tk = min(n, BOUNDED_KEY_BLOCK)
    subtiles = max(1, BOUNDED_UNROLLED_TILES // (n // tk))
    ot = lax.cond(p["bounded"],
                  functools.partial(attn, tq=min(n, subtiles * BOUNDED_Q_SUBTILE), tk=tk, bounded=True),
                  functools.partial(attn, tq=min(n, 256), tk=min(n, 512), bounded=False))

    four = _seq_fft(xv, f_m[0, :N_META], *_seq_fft_tables(n))

    return _mix_ffn(x, ot, four, gates, p["w_attn_o"], p["w_four_o"], p["w_out"], p["norm_ffn_g"],
                    p["w_ffn_in"], p["w_ffn_out"], p["final_norm_g"], tm=min(n, MIX_FFN_TILE),
                    ff_chunk=p["w_ffn_out"].shape[0])


def kernel(x_prompt, x_sample, meta_tokens, norm_mix_g, w_in, q_norm_g, k_norm_g, w_attn_o, w_four_o, w_out,
           norm_ffn_g, w_ffn_in, w_ffn_out, final_norm_g):
    assert w_in.shape[0] == 1, "single-layer encoder: meta rows are only needed as keys / DFT inputs"
    reps = LANES // HEAD_DIM
    head_id = jnp.arange(2 * LANES) // HEAD_DIM
    bound = 1.01 * LOG2E * HEAD_DIM ** 0.5 * jnp.max(jnp.abs(q_norm_g[0])) * jnp.max(jnp.abs(k_norm_g[0]))
    bounded = bound <= MAX_SCORE_BOUND_LOG2
    p = {
        "bounded": bounded,
        "kbias": jnp.where((jnp.arange(LANES) == HEAD_DIM) & bounded, -bound, 0.0).astype(F32)[None, :],
        "norm_mix_g": norm_mix_g[0][None, :],
        "w_in": w_in[0].astype(BF16),
        "gq": q_norm_g[0],
        "gk": jnp.tile(k_norm_g[0], reps)[None, :],
        "hmean": ((head_id[:, None] == head_id[None, :]).astype(F32) / HEAD_DIM).astype(BF16),
        "w_attn_o": w_attn_o[0].astype(BF16),
        "w_four_o": w_four_o[0].astype(BF16),
        "w_out": w_out[0].astype(BF16),
        "norm_ffn_g": norm_ffn_g[0][None, :],
        "w_ffn_in": w_ffn_in[0].astype(BF16),
        "w_ffn_out": w_ffn_out[0].astype(BF16),
        "final_norm_g": final_norm_g[None, :],
    }
    return (_encode(x_prompt, meta_tokens, p), _encode(x_sample, meta_tokens, p))
```

```python
import functools
import math

import jax
import jax.numpy as jnp
from jax import lax
from jax.experimental import pallas as pl
from jax.experimental.pallas import tpu as pltpu

N_META = 16
GRID_W = 64
HEAD_DIM = 64
N_Q_HEADS = 8
N_KV_HEADS = 2
Q_PER_KV = N_Q_HEADS // N_KV_HEADS
ATTN_WIDTH = N_Q_HEADS * HEAD_DIM
KV_WIDTH = N_KV_HEADS * HEAD_DIM
FOURIER_WIDTH = 512
FOURIER_GROUP = 128
N_FOURIER_GROUPS = FOURIER_WIDTH // FOURIER_GROUP
ROPE_AXIS_DIM = HEAD_DIM // 2
ROPE_THETA = 10000.0
EPS = 1e-6

LANES = 128
VMEM_LIMIT_BYTES = 56 * 1024 * 1024

F32 = jnp.float32
BF16 = jnp.bfloat16

LOG2E = math.log2(math.e)
Q_SCALE = HEAD_DIM ** -0.5 * LOG2E
MAX_SCORE_BOUND_LOG2 = 60.0
BOUNDED_Q_SUBTILE = 2 * LANES
BOUNDED_KEY_BLOCK = 2 * LANES
BOUNDED_UNROLLED_TILES = 64

FFT_RADIX = N_META
FFT_A_RESIDUES_PER_STEP = 4
FFT_A_SMALL_MATRIX = 1 << 16

IN_PROJ_TILE = 512
IN_PROJ_SUBTILE = 512
MIX_FFN_TILE = 512
MIX_FFN_SUBTILE = 256


def _dot(a, b):
    return jnp.dot(a, b, preferred_element_type=F32)


def _const_spec(shape):
    zeros = (0,) * len(shape)
    return pl.BlockSpec(shape, lambda *_: zeros, pipeline_mode=pl.Buffered(1))


def _params(n_axes):
    return pltpu.CompilerParams(dimension_semantics=("arbitrary",) * n_axes,
                                vmem_limit_bytes=VMEM_LIMIT_BYTES)


def _in_proj_kernel(x_ref, g_ref, w_ref, cos_ref, sin_ref, qa_ref, qb_ref, gk_ref, kbias_ref, hmean_ref,
                    qt_ref, k_ref, vt_ref, z_ref, gate_ref, *slab_ref):
    tm, d_model = x_ref.shape[1:]
    sub = min(tm, IN_PROJ_SUBTILE)
    o0 = ATTN_WIDTH
    o1 = o0 + KV_WIDTH
    o2 = o1 + KV_WIDTH
    o3 = o2 + FOURIER_WIDTH
    lane = lax.broadcasted_iota(jnp.int32, (sub, LANES), 1)
    first_half = (lane % ROPE_AXIS_DIM) < (ROPE_AXIS_DIM // 2)
    low = lane < HEAD_DIM

    def head_norm(z2):
        return z2 * lax.rsqrt(_dot((z2 * z2).astype(BF16), hmean_ref[...]) + EPS)

    spans = [slice(r, r + sub) for r in range(0, tm, sub)]
    subs = range(len(spans))
    hs = []
    for rows in spans:
        x = x_ref[0, rows, :]
        hs.append((x * lax.rsqrt(jnp.mean(x * x, axis=-1, keepdims=True) + EPS) * g_ref[...]).astype(BF16))

    def rope(y, rows):
        half = ROPE_AXIS_DIM // 2
        partner = jnp.where(first_half, pltpu.roll(y, LANES - half, 1), pltpu.roll(y, half, 1))
        return y * cos_ref[rows, :] + partner * sin_ref[rows, :]

    zq = [_dot(hs[s], w_ref[:, :o0]) for s in subs]
    quarter = ROPE_AXIS_DIM // 2
    for s, rows in enumerate(spans):
        for c in range(ATTN_WIDTH // LANES):
            zt = zq[s][:, c * LANES:(c + 1) * LANES].T
            for hh in range(LANES // HEAD_DIM):
                blk = zt[hh * HEAD_DIM:(hh + 1) * HEAD_DIM, :]
                y = blk * lax.rsqrt(jnp.mean(blk * blk, axis=0, keepdims=True) + EPS)
                partner = jnp.concatenate([y[quarter:2 * quarter], y[:quarter], y[3 * quarter:], y[2 * quarter:3 * quarter]],
                                          axis=0)
                r0 = c * LANES + hh * HEAD_DIM
                qt_ref[0, r0:r0 + HEAD_DIM, rows] = (y * qa_ref[:, rows] + partner * qb_ref[:, rows]).astype(BF16)

    zkv = [_dot(hs[s], w_ref[:, o0:o2]) for s in subs]
    kbias = kbias_ref[...]
    for s, rows in enumerate(spans):
        yk = rope(head_norm(zkv[s])[:, :KV_WIDTH] * gk_ref[...], rows)
        k_ref[0, 0, rows, :] = jnp.where(low, yk, kbias).astype(BF16)
        k_ref[0, 1, rows, :] = jnp.where(low, pltpu.roll(yk, HEAD_DIM, 1), kbias).astype(BF16)
        vt_ref[0, :, rows] = zkv[s][:, KV_WIDTH:].T.astype(BF16)

    for s, rows in enumerate(spans):
        zf = _dot(hs[s], w_ref[:, o2:o3])
        if not slab_ref:
            z_ref[0, rows, :] = zf.astype(BF16)
        else:
            slabs, = slab_ref
            n_rows = sub // FFT_RADIX
            for g in range(N_FOURIER_GROUPS):
                slabs[s, g] = zf[:, g * FOURIER_GROUP:(g + 1) * FOURIER_GROUP]
            for j in range(FFT_RADIX):
                for g in range(N_FOURIER_GROUPS):
                    lo = j * FOURIER_WIDTH + g * FOURIER_GROUP
                    z_ref[0, s * n_rows:(s + 1) * n_rows, lo:lo + FOURIER_GROUP] = (
                        slabs[s, g, pl.ds(j, n_rows, stride=FFT_RADIX), :].astype(BF16))

    gate_chunk = 512
    for c in range(2 * d_model // gate_chunk):
        for s, rows in enumerate(spans):
            zg = _dot(hs[s], w_ref[:, o3 + c * gate_chunk:o3 + (c + 1) * gate_chunk])
            gate_ref[0, rows, c * gate_chunk:(c + 1) * gate_chunk] = (1.0 / (1.0 + jnp.exp(-zg))).astype(BF16)


def _in_proj(x, norm_g, w_in, cos, sin, qa, qb, gk, kbias, hmean, tm, residue_major):
    b, n, d = x.shape
    in_width = w_in.shape[1]
    grid = (b, n // tm)
    if residue_major:
        four_rows, four_tile, four_width = n // FFT_RADIX, tm // FFT_RADIX, FFT_RADIX * FOURIER_WIDTH
        sub = min(tm, IN_PROJ_SUBTILE)
        scratch = [pltpu.VMEM((tm // sub, N_FOURIER_GROUPS, sub, FOURIER_GROUP), F32)]
    else:
        four_rows, four_tile, four_width = n, tm, FOURIER_WIDTH
        scratch = []
    out_shape = (
        jax.ShapeDtypeStruct((b, ATTN_WIDTH, n), BF16),
        jax.ShapeDtypeStruct((b, N_KV_HEADS, n, LANES), BF16),
        jax.ShapeDtypeStruct((b, KV_WIDTH, n), BF16),
        jax.ShapeDtypeStruct((b, four_rows, four_width), BF16),
        jax.ShapeDtypeStruct((b, n, 2 * d), BF16),
    )
    return pl.pallas_call(
        _in_proj_kernel,
        grid=grid,
        in_specs=[
            pl.BlockSpec((1, tm, d), lambda i, j: (i, j, 0)),
            _const_spec((1, d)),
            _const_spec((d, in_width)),
            pl.BlockSpec((tm, LANES), lambda i, j: (j, 0)),
            pl.BlockSpec((tm, LANES), lambda i, j: (j, 0)),
            pl.BlockSpec((HEAD_DIM, tm), lambda i, j: (0, j)),
            pl.BlockSpec((HEAD_DIM, tm), lambda i, j: (0, j)),
            _const_spec((1, LANES)),
            _const_spec((1, LANES)),
            _const_spec((2 * LANES, 2 * LANES)),
        ],
        out_specs=(
            pl.BlockSpec((1, ATTN_WIDTH, tm), lambda i, j: (i, 0, j)),
            pl.BlockSpec((1, N_KV_HEADS, tm, LANES), lambda i, j: (i, 0, j, 0)),
            pl.BlockSpec((1, KV_WIDTH, tm), lambda i, j: (i, 0, j)),
            pl.BlockSpec((1, four_tile, four_width), lambda i, j: (i, j, 0)),
            pl.BlockSpec((1, tm, 2 * d), lambda i, j: (i, j, 0)),
        ),
        out_shape=out_shape,
        scratch_shapes=scratch,
        compiler_params=_params(2),
        name="in_proj",
    )(x, norm_g, w_in, cos, sin, qa, qb, gk, kbias, hmean)


def _attn_kernel(qt_ref, k_ref, vt_ref, km_ref, vtm_ref, ot_ref, *, tk, bounded):
    tq = qt_ref.shape[2]
    n_keys = k_ref.shape[2]
    n_blocks = n_keys // tk
    heads = range(Q_PER_KV)

    def head_qt(h, lo, width):
        ones_row = (lax.broadcasted_iota(jnp.int32, (HEAD_DIM, width), 0) == 0).astype(BF16)
        return jnp.concatenate([qt_ref[0, h * HEAD_DIM:(h + 1) * HEAD_DIM, lo:lo + width], ones_row], axis=0)

    if bounded:
        sub = min(tq, BOUNDED_Q_SUBTILE)
        for lo in range(0, tq, sub):
            qts = [head_qt(h, lo, sub) for h in heads]
            p = [jnp.exp2(_dot(km_ref[0, 0], qts[h])) for h in heads]
            l = [jnp.sum(p[h], axis=0, keepdims=True) for h in heads]
            acc = [_dot(vtm_ref[0], p[h].astype(BF16)) for h in heads]
            s = [_dot(k_ref[0, 0, 0:tk, :], qts[h]) for h in heads]
            for j in range(n_blocks):
                for h in heads:
                    ph = jnp.exp2(s[h])
                    l[h] = l[h] + jnp.sum(ph, axis=0, keepdims=True)
                    if j + 1 < n_blocks:
                        s[h] = _dot(k_ref[0, 0, (j + 1) * tk:(j + 2) * tk, :], qts[h])
                    acc[h] = acc[h] + _dot(vt_ref[0, :, j * tk:(j + 1) * tk], ph.astype(BF16))
            for h in heads:
                ot_ref[0, h * HEAD_DIM:(h + 1) * HEAD_DIM, lo:lo + sub] = (acc[h] / l[h]).astype(BF16)
        return

    qt = jnp.concatenate([head_qt(h, 0, tq) for h in heads], axis=1)

    def block(kb, vb, m, l, acc):
        s = _dot(kb, qt)
        m_new = jnp.maximum(m, jnp.max(s, axis=0, keepdims=True))
        alpha = jnp.exp2(m - m_new)
        p = jnp.exp2(s - m_new)
        return (m_new, alpha * l + jnp.sum(p, axis=0, keepdims=True), alpha * acc + _dot(vb, p.astype(BF16)))

    width = qt.shape[1]
    init = (jnp.full((1, width), -jnp.inf, F32), jnp.zeros((1, width), F32), jnp.zeros((HEAD_DIM, width), F32))
    carry = block(km_ref[0, 0], vtm_ref[0], *init)

    def body(j, carry):
        start = pl.multiple_of(j * tk, tk)
        return block(k_ref[0, 0, pl.ds(start, tk), :], vt_ref[0, :, pl.ds(start, tk)], *carry)

    _, l, acc = lax.fori_loop(0, n_blocks, body, carry)
    out = (acc / l).astype(BF16)
    for h in range(Q_PER_KV):
        ot_ref[0, h * HEAD_DIM:(h + 1) * HEAD_DIM, :] = out[:, h * tq:(h + 1) * tq]


def _attention(qt, k, vt, k_meta, vt_meta, tq, tk, bounded):
    b, _, n = qt.shape
    group_rows = Q_PER_KV * HEAD_DIM
    return pl.pallas_call(
        functools.partial(_attn_kernel, tk=tk, bounded=bounded),
        grid=(b, N_KV_HEADS, n // tq),
        in_specs=[
            pl.BlockSpec((1, group_rows, tq), lambda i, g, j: (i, g, j)),
            pl.BlockSpec((1, 1, n, LANES), lambda i, g, j: (i, g, 0, 0)),
            pl.BlockSpec((1, HEAD_DIM, n), lambda i, g, j: (i, g, 0)),
            pl.BlockSpec((1, 1, N_META, LANES), lambda i, g, j: (0, g, 0, 0)),
            pl.BlockSpec((1, HEAD_DIM, N_META), lambda i, g, j: (0, g, 0)),
        ],
        out_specs=pl.BlockSpec((1, group_rows, tq), lambda i, g, j: (i, g, j)),
        out_shape=jax.ShapeDtypeStruct((b, ATTN_WIDTH, n), BF16),
        compiler_params=_params(3),
        name="attention",
    )(qt, k, vt, k_meta, vt_meta)


def _fft_a_kernel(e_ref, x_ref, t_ref, xm_ref, out_ref, meta_ref):
    n1p = out_ref.shape[1]
    width = FOURIER_WIDTH
    residues = e_ref.shape[0]

    @pl.when(pl.program_id(1) == 0)
    def _():
        for q in range(residues):
            meta_ref[q] = _dot(t_ref[q], xm_ref[...])

    for q in range(residues):
        r = _dot(e_ref[q], x_ref[0, :, q * width:(q + 1) * width]) + meta_ref[q]
        for g in range(N_FOURIER_GROUPS):
            base = (q * N_FOURIER_GROUPS + g) * 2 * FOURIER_GROUP
            cols = slice(g * FOURIER_GROUP, (g + 1) * FOURIER_GROUP)
            out_ref[0, :, base:base + FOURIER_GROUP] = r[:n1p, cols].astype(BF16)
            out_ref[0, :, base + FOURIER_GROUP:base + 2 * FOURIER_GROUP] = r[n1p:, cols].astype(BF16)


def _times_unit(re, im, turn):
    quarter = turn * 4.0
    if quarter == round(quarter):
        return [(re, im), (im, -re), (-re, -im), (-im, re)][int(round(quarter)) % 4]
    c, s = math.cos(2.0 * math.pi * turn), -math.sin(2.0 * math.pi * turn)
    return re * c - im * s, re * s + im * c


def _fft_real_part(xs):
    def fft(seq):
        n = len(seq)
        if n == 1:
            return seq
        even, odd = fft(seq[0::2]), fft(seq[1::2])
        out = [None] * n
        for k in range(n // 2):
            tr, ti = _times_unit(*odd[k], k / n)
            out[k] = (even[k][0] + tr, even[k][1] + ti)
            out[k + n // 2] = (even[k][0] - tr, even[k][1] - ti)
        return out

    n = len(xs)
    even, odd = fft(xs[0::2]), fft(xs[1::2])
    out = [None] * n
    for k in range(n // 2):
        tr, _ = _times_unit(*odd[k], k / n)
        out[k] = even[k][0] + tr
        out[k + n // 2] = even[k][0] - tr
    return out


def _fft_b_kernel(ab_ref, mc_ref, out_ref):
    rows = ab_ref.shape[1]
    per_residue = N_FOURIER_GROUPS * 2 * FOURIER_GROUP
    for g in range(N_FOURIER_GROUPS):
        lo = g * 2 * FOURIER_GROUP
        lhs = jnp.concatenate([ab_ref[0, :, q * per_residue + lo:q * per_residue + lo + 2 * FOURIER_GROUP]
                               for q in range(FFT_RADIX)], axis=0)
        c = _dot(lhs, mc_ref[...])
        xs = [(c[q * rows:(q + 1) * rows, :FOURIER_GROUP], c[q * rows:(q + 1) * rows, FOURIER_GROUP:])
              for q in range(FFT_RADIX)]
        for k2, y in enumerate(_fft_real_part(xs)):
            out_ref[0, k2, :, g * FOURIER_GROUP:(g + 1) * FOURIER_GROUP] = y.astype(BF16)


def _fft_row_tile(n1p):
    return max(t for t in range(16, 193, 16) if n1p % t == 0)


def _seq_fft(xv, f_meta, e, t, mc):
    b, m, _ = xv.shape
    width = FOURIER_WIDTH
    n1p = e.shape[1] // 2
    qs = FFT_RADIX if 2 * n1p * m <= FFT_A_SMALL_MATRIX else FFT_A_RESIDUES_PER_STEP
    per_residue = N_FOURIER_GROUPS * 2 * FOURIER_GROUP
    ab = pl.pallas_call(
        _fft_a_kernel,
        grid=(FFT_RADIX // qs, b),
        in_specs=[
            pl.BlockSpec((qs, 2 * n1p, m), lambda q, i: (q, 0, 0)),
            pl.BlockSpec((1, m, qs * width), lambda q, i: (i, 0, q)),
            pl.BlockSpec((qs, 2 * n1p, FFT_RADIX), lambda q, i: (q, 0, 0)),
            pl.BlockSpec((FFT_RADIX, width), lambda q, i: (0, 0)),
        ],
        out_specs=pl.BlockSpec((1, n1p, qs * per_residue), lambda q, i: (i, 0, q)),
        out_shape=jax.ShapeDtypeStruct((b, n1p, FFT_RADIX * per_residue), BF16),
        scratch_shapes=[pltpu.VMEM((qs, 2 * n1p, width), F32)],
        compiler_params=_params(2),
        name="seq_fft_a",
    )(e, xv, t, f_meta)
    tr = _fft_row_tile(n1p)
    return pl.pallas_call(
        _fft_b_kernel,
        grid=(b, n1p // tr),
        in_specs=[
            pl.BlockSpec((1, tr, FFT_RADIX * per_residue), lambda i, r: (i, r, 0)),
            _const_spec(mc.shape),
        ],
        out_specs=pl.BlockSpec((1, FFT_RADIX, tr, width), lambda i, r: (i, 0, r, 0)),
        out_shape=jax.ShapeDtypeStruct((b, FFT_RADIX, n1p, width), BF16),
        compiler_params=_params(2),
        name="seq_fft_b",
    )(ab, mc)


def _seq_fft_tables(n):
    length = n + N_META
    m = n // FFT_RADIX
    n1 = m + 1
    n1p = -(-n1 // 16) * 16
    k1 = jnp.arange(n1p, dtype=jnp.int32)
    n2 = jnp.arange(FFT_RADIX, dtype=jnp.int32)
    valid = (k1 < n1).astype(F32)
    c1, s1 = _unit_circle(k1[:, None] * (FFT_RADIX * (jnp.arange(m, dtype=jnp.int32) + 1))[None, :], length)
    c2, s2 = _unit_circle(n2[:, None] * k1[None, :], length)
    c2, s2 = c2 * valid[None, :], s2 * valid[None, :]
    cos = c1[None, :, :] * c2[:, :, None] - s1[None, :, :] * s2[:, :, None]
    sin = s1[None, :, :] * c2[:, :, None] + c1[None, :, :] * s2[:, :, None]
    e = jnp.concatenate([cos, -sin], axis=1).astype(BF16)
    pick = (n2[:, None, None] == n2[None, None, :]).astype(F32)
    t = jnp.concatenate([c2[:, :, None] * pick, -s2[:, :, None] * pick], axis=1).astype(BF16)
    ch = jnp.arange(FOURIER_GROUP, dtype=jnp.int32)
    cb, sb = _unit_circle(ch[:, None] * ch[None, :], FOURIER_GROUP)
    scale = 1.0 / math.sqrt(length * FOURIER_GROUP)
    mc = jnp.concatenate([jnp.concatenate([cb, -sb], axis=1), jnp.concatenate([sb, cb], axis=1)], axis=0) * scale
    return e, t, mc.astype(BF16)


def _rms(x, g):
    return x * lax.rsqrt(jnp.mean(x * x, axis=-1, keepdims=True) + EPS) * g


def _slab_rows(slabs_ref, first_pos, tm, n1):
    n1p = slabs_ref.shape[2]
    window = min(FFT_RADIX, -(-(tm - 1) // n1) + 1)
    span = min(n1p, -(-(tm + 15) // 16) * 16)
    first_slab = first_pos // n1
    rel0 = first_pos - first_slab * n1
    row0 = pl.multiple_of(jnp.minimum((rel0 // 16) * 16, n1p - span), 16)
    rel = lax.broadcasted_iota(jnp.int32, (tm, 1), 0) + rel0
    target = rel - row0
    for w in range(1, window):
        target = target + jnp.where(rel >= w * n1, span - n1 + (row0 if w == 1 else 0), 0)
    col = lax.broadcasted_iota(jnp.int32, (tm, window * span), 1)
    pick = (col == target).astype(BF16)
    rows = jnp.concatenate([slabs_ref[0, first_slab, pl.ds(row0, span), :]]
                           + [slabs_ref[0, jnp.minimum(first_slab + w, FFT_RADIX - 1), 0:span, :]
                              for w in range(1, window)], axis=0)
    return _dot(pick, rows).astype(BF16)


def _mix_ffn_kernel(x_ref, ot_ref, gate_ref, slabs_ref, wao_ref, wfo_ref, wout_ref, g2_ref, wfi_ref, wfo2_ref,
                    gfin_ref, y_ref, *, ff_chunk, slab_n1):
    tm, d_model = x_ref.shape[1:]
    d_ff = wfo2_ref.shape[0]
    sub = min(tm, MIX_FFN_SUBTILE)
    spans = [slice(r, r + sub) for r in range(0, tm, sub)]
    a, fo, x1, h2 = [], [], [], []
    for rows in spans:
        attn = ot_ref[0, :, rows].astype(F32).T.astype(BF16)
        a.append(_dot(attn, wao_ref[...]))
        four = _slab_rows(slabs_ref, N_META + pl.program_id(1) * tm + rows.start, sub, slab_n1)
        fo.append(_dot(four, wfo_ref[...]))
    for s, rows in enumerate(spans):
        gates = gate_ref[0, rows, :]
        merged = gates[:, :d_model].astype(F32) * a[s] + gates[:, d_model:].astype(F32) * fo[s]
        x1.append(x_ref[0, rows, :] + _dot(merged.astype(BF16), wout_ref[...]))
    for s in range(len(spans)):
        h2.append(_rms(x1[s], g2_ref[...]).astype(BF16))
    for s, rows in enumerate(spans):
        acc = x1[s]
        for c in range(d_ff // ff_chunk):
            gt = _dot(h2[s], wfi_ref[:, c * ff_chunk:(c + 1) * ff_chunk])
            up = _dot(h2[s], wfi_ref[:, d_ff + c * ff_chunk:d_ff + (c + 1) * ff_chunk])
            act = gt * (1.0 / (1.0 + jnp.exp(-gt))) * up
            acc = acc + _dot(act.astype(BF16), wfo2_ref[c * ff_chunk:(c + 1) * ff_chunk, :])
        y_ref[0, rows, :] = _rms(acc, gfin_ref[...])


def _mix_ffn(x, ot, four, gates, w_attn_o, w_four_o, w_out, g2, w_ffn_in, w_ffn_out, gfin, tm, ff_chunk):
    b, n, d = x.shape
    d_ff = w_ffn_out.shape[0]
    slab_n1 = n // FFT_RADIX + 1
    four_spec = pl.BlockSpec((1,) + four.shape[1:], lambda i, j: (i, 0, 0, 0), pipeline_mode=pl.Buffered(1))
    return pl.pallas_call(
        functools.partial(_mix_ffn_kernel, ff_chunk=ff_chunk, slab_n1=slab_n1),
        grid=(b, n // tm),
        in_specs=[
            pl.BlockSpec((1, tm, d), lambda i, j: (i, j, 0)),
            pl.BlockSpec((1, ATTN_WIDTH, tm), lambda i, j: (i, 0, j)),
            pl.BlockSpec((1, tm, 2 * d), lambda i, j: (i, j, 0)),
            four_spec,
            _const_spec((ATTN_WIDTH, d)),
            _const_spec((FOURIER_WIDTH, d)),
            _const_spec((d, d)),
            _const_spec((1, d)),
            _const_spec((d, 2 * d_ff)),
            _const_spec((d_ff, d)),
            _const_spec((1, d)),
        ],
        out_specs=pl.BlockSpec((1, tm, d), lambda i, j: (i, j, 0)),
        out_shape=jax.ShapeDtypeStruct((b, n, d), F32),
        compiler_params=_params(2),
        name="mix_ffn",
    )(x, ot, gates, four, w_attn_o, w_four_o, w_out, g2, w_ffn_in, w_ffn_out, gfin)


def _rope_tables(row, col):
    inv_freq = 1.0 / (ROPE_THETA ** (jnp.arange(0, ROPE_AXIS_DIM, 2, dtype=F32) / ROPE_AXIS_DIM))
    ar = row[:, None] * inv_freq[None, :]
    ac = col[:, None] * inv_freq[None, :]
    cos = jnp.concatenate([jnp.cos(ar), jnp.cos(ar), jnp.cos(ac), jnp.cos(ac)], axis=-1)
    sin = jnp.concatenate([-jnp.sin(ar), jnp.sin(ar), -jnp.sin(ac), jnp.sin(ac)], axis=-1)
    reps = LANES // HEAD_DIM
    return jnp.tile(cos, (1, reps)), jnp.tile(sin, (1, reps))


def _q_rope_tables(row, col, gq):
    inv_freq = 1.0 / (ROPE_THETA ** (jnp.arange(0, ROPE_AXIS_DIM, 2, dtype=F32) / ROPE_AXIS_DIM))
    ar = inv_freq[:, None] * row[None, :]
    ac = inv_freq[:, None] * col[None, :]
    cos = jnp.concatenate([jnp.cos(ar), jnp.cos(ar), jnp.cos(ac), jnp.cos(ac)], axis=0)
    sin = jnp.concatenate([-jnp.sin(ar), jnp.sin(ar), -jnp.sin(ac), jnp.sin(ac)], axis=0)
    partner = jnp.arange(HEAD_DIM) ^ (ROPE_AXIS_DIM // 2)
    return cos * (gq * Q_SCALE)[:, None], sin * (gq[partner] * Q_SCALE)[:, None]


def _unit_circle(prod, length):
    ang = (prod % length).astype(F32) * (2.0 * math.pi / length)
    return jnp.cos(ang), jnp.sin(ang)


def _encode(x, meta_tokens, p):
    b, n, d = x.shape
    tok = jnp.arange(n, dtype=jnp.int32)
    row, col = (tok // GRID_W).astype(F32), (tok % GRID_W).astype(F32)
    meta_pad = LANES
    mrow = jnp.full((meta_pad,), -1.0, F32)
    mcol = jnp.arange(meta_pad, dtype=F32)
    x_meta = jnp.zeros((1, meta_pad, d), F32).at[0, :N_META].set(meta_tokens)

    consts = (p["norm_mix_g"], p["w_in"])
    head_consts = (p["gk"], p["kbias"], p["hmean"])
    _, k_m, vt_m, f_m, _ = _in_proj(x_meta, *consts, *_rope_tables(mrow, mcol), *_q_rope_tables(mrow, mcol, p["gq"]),
                                    *head_consts, tm=meta_pad, residue_major=False)
    qt, k, vt, xv, gates = _in_proj(x, *consts, *_rope_tables(row, col), *_q_rope_tables(row, col, p["gq"]),
                                    *head_consts, tm=min(n, IN_PROJ_TILE), residue_major=True)

    attn = functools.partial(_attention, qt, k, vt, k_m[:, :, :N_META], vt_m[:, :, :N_META])
    tk = min(n, BOUNDED_KEY_BLOCK)
    subtiles = max(1, BOUNDED_UNROLLED_TILES // (n // tk))
    ot = lax.cond(p["bounded"],
                  functools.partial(attn, tq=min(n, subtiles * BOUNDED_Q_SUBTILE), tk=tk, bounded=True),
                  functools.partial(attn, tq=min(n, 256), tk=min(n, 512), bounded=False))

    four = _seq_fft(xv, f_m[0, :N_META], *_seq_fft_tables(n))

    return _mix_ffn(x, ot, four, gates, p["w_attn_o"], p["w_four_o"], p["w_out"], p["norm_ffn_g"],
                    p["w_ffn_in"], p["w_ffn_out"], p["final_norm_g"], tm=min(n, MIX_FFN_TILE),
                    ff_chunk=p["w_ffn_out"].shape[0])


def kernel(x_prompt, x_sample, meta_tokens, norm_mix_g, w_in, q_norm_g, k_norm_g, w_attn_o, w_four_o, w_out,
           norm_ffn_g, w_ffn_in, w_ffn_out, final_norm_g):
    assert w_in.shape[0] == 1, "single-layer encoder: meta rows are only needed as keys / DFT inputs"
    reps = LANES // HEAD_DIM
    head_id = jnp.arange(2 * LANES) // HEAD_DIM
    bound = 1.01 * LOG2E * HEAD_DIM ** 0.5 * jnp.max(jnp.abs(q_norm_g[0])) * jnp.max(jnp.abs(k_norm_g[0]))
    bounded = bound <= MAX_SCORE_BOUND_LOG2
    p = {
        "bounded": bounded,
        "kbias": jnp.where((jnp.arange(LANES) == HEAD_DIM) & bounded, -bound, 0.0).astype(F32)[None, :],
        "norm_mix_g": norm_mix_g[0][None, :],
        "w_in": w_in[0].astype(BF16),
        "gq": q_norm_g[0],
        "gk": jnp.tile(k_norm_g[0], reps)[None, :],
        "hmean": ((head_id[:, None] == head_id[None, :]).astype(F32) / HEAD_DIM).astype(BF16),
        "w_attn_o": w_attn_o[0].astype(BF16),
        "w_four_o": w_four_o[0].astype(BF16),
        "w_out": w_out[0].astype(BF16),
        "norm_ffn_g": norm_ffn_g[0][None, :],
        "w_ffn_in": w_ffn_in[0].astype(BF16),
        "w_ffn_out": w_ffn_out[0].astype(BF16),
        "final_norm_g": final_norm_g[None, :],
    }
    return (_encode(x_prompt, meta_tokens, p), _encode(x_sample, meta_tokens, p))
```
